```python
import math
import jax, jax.numpy as jnp
from jax import lax
import numpy as np

D_MODEL = 1024
BATCH = 32
SEQ = 2048
DEPTH = 1

N_META = 16
MIX_WIDTH = D_MODEL
ATTN_WIDTH = MIX_WIDTH // 2
POOL_WIDTH = MIX_WIDTH - ATTN_WIDTH
ATTN_HEAD_DIM = 64
ATTN_HEADS = ATTN_WIDTH // (2 * ATTN_HEAD_DIM)
ATTN_VALUE_DIM = 2 * ATTN_HEAD_DIM
POOL_WINDOWS = (2, 4, 8, 16)
POOL_GROUPS = len(POOL_WINDOWS)
POOL_GROUP_DIM = POOL_WIDTH // POOL_GROUPS
IN_WIDTH = 3 * ATTN_WIDTH + POOL_WIDTH
N_EXPERTS = 16
EXPERT_CAPACITY_FACTOR = 2
D_FF = 2048
ROPE_THETA = 10000.0
Q_BLOCK = 128
EPS = 1e-6

kernel_name = "hymba_diffattn_pool_ecmoe_block"


def _rms(x, g):
    x32 = x.astype(jnp.float32)
    y = x32 * lax.rsqrt(jnp.mean(x32 * x32, axis=-1, keepdims=True) + EPS)
    return (y * g.astype(jnp.float32)).astype(x.dtype)


def _rope(x, pos):
    half = x.shape[-1] // 2
    freqs = ROPE_THETA ** (-jnp.arange(half, dtype=jnp.float32) / half)
    ang = pos[:, None] * freqs[None, :]
    cos = jnp.cos(ang).astype(x.dtype)
    sin = jnp.sin(ang).astype(x.dtype)
    x1, x2 = x[..., :half], x[..., half:]
    return jnp.concatenate([x1 * cos - x2 * sin, x1 * sin + x2 * cos], axis=-1)


def _diff_attention(q, k, v, lam, lam_init, subln_g):
    B, L = q.shape[0], q.shape[1]
    H, dh, dv = ATTN_HEADS, ATTN_HEAD_DIM, ATTN_VALUE_DIM
    pos = jnp.arange(L, dtype=jnp.float32)
    q = _rope(jnp.transpose(q, (0, 2, 3, 1, 4)), pos)
    k = _rope(jnp.transpose(k, (0, 2, 3, 1, 4)), pos)
    v = jnp.transpose(v, (0, 2, 1, 3))
    n_blk = -(-L // Q_BLOCK)
    Lp = n_blk * Q_BLOCK
    q = jnp.pad(q, ((0, 0), (0, 0), (0, 0), (0, Lp - L), (0, 0)))
    qb = jnp.moveaxis(q.reshape(B, H, 2, n_blk, Q_BLOCK, dh), 3, 0)
    scale = dh ** -0.5

    def one_block(qblk):
        s = jnp.einsum('bhmqd,bhmkd->bhmqk', qblk, k).astype(jnp.float32) * scale
        p = jax.nn.softmax(s, axis=-1)
        a = p[:, :, 0] - lam * p[:, :, 1]
        return jnp.einsum('bhqk,bhkd->bhqd', a.astype(v.dtype), v)

    o = lax.map(one_block, qb)
    o = jnp.moveaxis(o, 0, 2).reshape(B, H, Lp, dv)[:, :, :L]
    o = _rms(o, subln_g) * (1.0 - lam_init)
    return jnp.transpose(o, (0, 2, 1, 3)).reshape(B, L, H * dv)


def _pool_mixer(p, pool_w, pool_scale):
    B, L, _ = p.shape
    p32 = p.astype(jnp.float32)
    cs = jnp.concatenate([jnp.zeros((B, 1, POOL_WIDTH), jnp.float32), jnp.cumsum(p32, axis=1)], axis=1)
    t = jnp.arange(L)
    outs = []
    for g, w in enumerate(POOL_WINDOWS):
        lo = jnp.clip(t - w // 2, 0, L)
        hi = jnp.clip(t + w // 2, 0, L)
        c0, c1 = g * POOL_GROUP_DIM, (g + 1) * POOL_GROUP_DIM
        csg = cs[:, :, c0:c1]
        mean = (csg[:, hi] - csg[:, lo]) / (hi - lo).astype(jnp.float32)[None, :, None]
        outs.append(mean - p32[:, :, c0:c1])
    d = jnp.stack(outs, axis=2).astype(p.dtype)
    y = jnp.einsum('blgc,gcd->blgd', d, pool_w)
    return y.reshape(B, L, POOL_WIDTH) * pool_scale


def _expert_choice_ffn(h, w_router, w_gate, w_up, w_down):
    B, L, D = h.shape
    cap = EXPERT_CAPACITY_FACTOR * L // N_EXPERTS
    aff = jax.nn.softmax(jnp.einsum('bld,de->ble', h, w_router).astype(jnp.float32), axis=-1)
    gates, idx = lax.top_k(jnp.swapaxes(aff, 1, 2), cap)
    xg = jax.vmap(lambda hb, ib: hb[ib])(h, idx)
    a = jax.nn.silu(jnp.einsum('becd,edf->becf', xg, w_gate)) * jnp.einsum('becd,edf->becf', xg, w_up)
    y = jnp.einsum('becf,efd->becd', a, w_down) * gates[..., None].astype(h.dtype)
    return jax.vmap(lambda ib, yb: jnp.zeros((L, D), yb.dtype).at[ib.reshape(-1)].add(yb.reshape(-1, D)))(idx, y)


def setup_inputs(seed: int = 0) -> dict:
    key = jax.random.key(seed)
    ks = jax.random.split(key, 20)
    f32 = jnp.float32
    nrm = lambda k, shape, s: jax.random.normal(k, shape, f32) * s
    gain = lambda k, shape: 1.0 + 0.01 * jax.random.normal(k, shape, f32)
    return {
        "x": nrm(ks[0], (BATCH, SEQ, D_MODEL), 1.0),
        "meta_tokens": nrm(ks[1], (N_META, D_MODEL), 1.0),
        "ln1_g": gain(ks[2], (DEPTH, D_MODEL)),
        "w_in": nrm(ks[3], (DEPTH, D_MODEL, IN_WIDTH), D_MODEL ** -0.5),
        "q_norm_g": gain(ks[4], (DEPTH, ATTN_HEAD_DIM)),
        "k_norm_g": gain(ks[5], (DEPTH, ATTN_HEAD_DIM)),
        "lambda_q1": nrm(ks[6], (DEPTH, ATTN_HEAD_DIM), 0.1),
        "lambda_k1": nrm(ks[7], (DEPTH, ATTN_HEAD_DIM), 0.1),
        "lambda_q2": nrm(ks[8], (DEPTH, ATTN_HEAD_DIM), 0.1),
        "lambda_k2": nrm(ks[9], (DEPTH, ATTN_HEAD_DIM), 0.1),
        "subln_g": gain(ks[10], (DEPTH, ATTN_VALUE_DIM)),
        "pool_w": nrm(ks[11], (DEPTH, POOL_GROUPS, POOL_GROUP_DIM, POOL_GROUP_DIM), POOL_GROUP_DIM ** -0.5),
        "pool_scale": gain(ks[12], (DEPTH, POOL_WIDTH)),
        "w_out": nrm(ks[13], (DEPTH, MIX_WIDTH, D_MODEL), MIX_WIDTH ** -0.5),
        "ln2_g": gain(ks[14], (DEPTH, D_MODEL)),
        "w_router": nrm(ks[15], (DEPTH, D_MODEL, N_EXPERTS), D_MODEL ** -0.5),
        "w_gate": nrm(ks[16], (DEPTH, N_EXPERTS, D_MODEL, D_FF), D_MODEL ** -0.5),
        "w_up": nrm(ks[17], (DEPTH, N_EXPERTS, D_MODEL, D_FF), D_MODEL ** -0.5),
        "w_down": nrm(ks[18], (DEPTH, N_EXPERTS, D_FF, D_MODEL), D_FF ** -0.5),
    }


def reference(x, meta_tokens, ln1_g, w_in, q_norm_g, k_norm_g, lambda_q1, lambda_k1, lambda_q2, lambda_k2,
              subln_g, pool_w, pool_scale, w_out, ln2_g, w_router, w_gate, w_up, w_down):
    B = x.shape[0]
    meta = jnp.broadcast_to(meta_tokens[None].astype(x.dtype), (B, N_META, D_MODEL))
    h = jnp.concatenate([meta, x], axis=1)
    L = h.shape[1]
    for l in range(DEPTH):
        lam_init = 0.8 - 0.6 * math.exp(-0.3 * l)
        hn = _rms(h, ln1_g[l])
        proj = jnp.einsum('bld,dn->bln', hn, w_in[l])
        q = proj[..., :ATTN_WIDTH].reshape(B, L, ATTN_HEADS, 2, ATTN_HEAD_DIM)
        k = proj[..., ATTN_WIDTH:2 * ATTN_WIDTH].reshape(B, L, ATTN_HEADS, 2, ATTN_HEAD_DIM)
        v = proj[..., 2 * ATTN_WIDTH:3 * ATTN_WIDTH].reshape(B, L, ATTN_HEADS, ATTN_VALUE_DIM)
        p = proj[..., 3 * ATTN_WIDTH:]
        q = _rms(q, q_norm_g[l])
        k = _rms(k, k_norm_g[l])
        lam = (jnp.exp(jnp.sum(lambda_q1[l].astype(jnp.float32) * lambda_k1[l].astype(jnp.float32)))
               - jnp.exp(jnp.sum(lambda_q2[l].astype(jnp.float32) * lambda_k2[l].astype(jnp.float32)))
               + lam_init)
        attn = _diff_attention(q, k, v, lam, lam_init, subln_g[l])
        pool = _pool_mixer(p, pool_w[l], pool_scale[l])
        mix = jnp.concatenate([attn, pool], axis=-1)
        h = h + jnp.einsum('blm,md->bld', mix, w_out[l])
        h = h + _expert_choice_ffn(_rms(h, ln2_g[l]), w_router[l], w_gate[l], w_up[l], w_down[l])
    return h[:, N_META:]
```

```python
import functools
import math

import jax
import jax.numpy as jnp
from jax import lax
from jax.experimental import pallas as pl
from jax.experimental.pallas import tpu as pltpu

F32 = jnp.float32
BF16 = jnp.bfloat16

D_MODEL = 1024
N_META = 16
ATTN_WIDTH = 512
POOL_WIDTH = 512
HEAD_DIM = 64
HEADS = 4
VALUE_DIM = 128
POOL_WINDOWS = (2, 4, 8, 16)
POOL_GROUP_DIM = 128
IN_WIDTH = 3 * ATTN_WIDTH + POOL_WIDTH
N_EXPERTS = 16
CAPACITY_FACTOR = 2
D_FF = 2048
ROPE_THETA = 10000.0
EPS = 1e-6
LAM_INIT = 0.8 - 0.6 * math.exp(-0.3 * 0)

LANES = 128
MXU_DIM = 256
SLOT_MAIN = MXU_DIM
SLOT_TAIL = 16
POOL_PAD = 32
SCORE_LOOKAHEAD = 3
VMEM_LIMIT = 56 * 1024 * 1024


def _row_tile(n, target):
    t = min(n, target)
    assert n % t == 0
    return t


def _cparams(n_axes):
    return pltpu.CompilerParams(dimension_semantics=("arbitrary",) * n_axes,
                                vmem_limit_bytes=VMEM_LIMIT)


def _dot(a, b):
    return jnp.dot(a, b, preferred_element_type=F32)


def _dot_nt(a, b):
    return lax.dot_general(a, b, (((1,), (1,)), ((), ())), preferred_element_type=F32)


def _inproj_kernel(x_ref, g1_ref, w_ref, wvt_ref, cos_ref, sin_ref, qg_ref, kg_ref, gm_ref,
                   q_ref, k_ref, vt_ref, p_ref):
    x = x_ref[0]
    ms = jnp.mean(x * x, axis=-1, keepdims=True)
    xn = (x * lax.rsqrt(ms + EPS) * g1_ref[...]).astype(BF16)
    cos = cos_ref[...]
    sin = sin_ref[...]
    gm = gm_ref[...]
    tm = x.shape[0]
    lane = lax.broadcasted_iota(jnp.int32, (tm, LANES), 1)
    low_half = (lane & 32) == 0

    def norm_rope(t, g, scale):
        sq = t * t
        hi = sq.astype(BF16)
        lo = (sq - hi.astype(F32)).astype(BF16)
        ss = _dot(hi, gm) + _dot(lo, gm)
        tn = t * lax.rsqrt(ss * (1.0 / HEAD_DIM) + EPS) * g
        sw = jnp.where(low_half, pltpu.roll(tn, LANES - 32, 1), pltpu.roll(tn, 32, 1))
        r = tn * cos + sw * sin
        if scale != 1.0:
            r = r * scale
        return r.astype(BF16)

    pq = _dot(xn, w_ref[:, 0:ATTN_WIDTH])
    for h in range(HEADS):
        q_ref[0, h] = norm_rope(pq[:, h * LANES:(h + 1) * LANES], qg_ref[...], HEAD_DIM ** -0.5 * math.log2(math.e))
    pk = _dot(xn, w_ref[:, ATTN_WIDTH:2 * ATTN_WIDTH])
    for h in range(HEADS):
        k_ref[0, h] = norm_rope(pk[:, h * LANES:(h + 1) * LANES], kg_ref[...], 1.0)
    pvt = _dot_nt(wvt_ref[...], xn)
    for h in range(HEADS):
        vt_ref[0, h] = pvt[h * VALUE_DIM:(h + 1) * VALUE_DIM, :].astype(BF16)
    p_ref[0] = _dot(xn, w_ref[:, 3 * ATTN_WIDTH:IN_WIDTH])


def _in_proj(x, g1, w_in, wvt, cos, sin, qg, kg, gm, tm):
    nb, n, _ = x.shape
    grid = (nb, n // tm)
    hd = jax.ShapeDtypeStruct((nb, HEADS, n, LANES), BF16)
    hdt = jax.ShapeDtypeStruct((nb, HEADS, VALUE_DIM, n), BF16)
    head_spec = pl.BlockSpec((1, HEADS, tm, LANES), lambda b, i: (b, 0, i, 0))
    headt_spec = pl.BlockSpec((1, HEADS, VALUE_DIM, tm), lambda b, i: (b, 0, 0, i))
    const = lambda shape: pl.BlockSpec(shape, lambda b, i: (0,) * len(shape))
    return pl.pallas_call(
        _inproj_kernel,
        grid=grid,
        in_specs=[
            pl.BlockSpec((1, tm, D_MODEL), lambda b, i: (b, i, 0)),
            const((1, D_MODEL)),
            const((D_MODEL, IN_WIDTH)),
            const((ATTN_WIDTH, D_MODEL)),
            pl.BlockSpec((tm, LANES), lambda b, i: (i, 0)),
            pl.BlockSpec((tm, LANES), lambda b, i: (i, 0)),
            const((1, LANES)),
            const((1, LANES)),
            const((LANES, LANES)),
        ],
        out_specs=[head_spec, head_spec, headt_spec,
                   pl.BlockSpec((1, tm, POOL_WIDTH), lambda b, i: (b, i, 0))],
        out_shape=[hd, hd, hdt, jax.ShapeDtypeStruct((nb, n, POOL_WIDTH), F32)],
        compiler_params=_cparams(2),
        name="in_proj",
    )(x, g1, w_in, wvt, cos, sin, qg, kg, gm)


def _attn_kernel(q_ref, kx_ref, km_ref, vxt_ref, vmt_ref, lq1_ref, lk1_ref, lq2_ref, lk2_ref,
                 sg_ref, o_ref):
    q = q_ref[0, 0]
    tq = q.shape[0]
    lane = lax.broadcasted_iota(jnp.int32, (tq, LANES), 1)
    zero = jnp.zeros_like(q)
    q_maps = (jnp.where(lane < HEAD_DIM, q, zero), jnp.where(lane >= HEAD_DIM, q, zero))
    kx = kx_ref[0, 0]
    km = km_ref[0, 0]
    vxt = vxt_ref[0, 0]
    vmt = vmt_ref[0, 0]
    tqs = MXU_DIM if tq % MXU_DIM == 0 else tq
    n_sub = tq // tqs
    meta_valid = lax.broadcasted_iota(jnp.int32, (LANES, tqs), 0) < N_META

    def scores(qq):
        sx = _dot_nt(kx, qq)
        sm = jnp.where(meta_valid, _dot_nt(km, qq), -1e30)
        m = jnp.maximum(jnp.max(sx, axis=0, keepdims=True), jnp.max(sm, axis=0, keepdims=True))
        return sx, sm, m

    def weighted_values(sx, sm, m):
        px = jnp.exp2(sx - m)
        pm = jnp.exp2(sm - m)
        l = jnp.sum(px, axis=0, keepdims=True) + jnp.sum(pm, axis=0, keepdims=True)
        ot = _dot(vxt, px.astype(BF16)) + _dot(vmt, pm.astype(BF16))
        return ot / l

    units = [(j, mp) for j in range(n_sub) for mp in range(2)]
    outs = {}
    pending = []
    for j, mp in units:
        pending.append(((j, mp), scores(q_maps[mp][j * tqs:(j + 1) * tqs])))
        if len(pending) > SCORE_LOOKAHEAD:
            key, args = pending.pop(0)
            outs[key] = weighted_values(*args)
    for key, args in pending:
        outs[key] = weighted_values(*args)

    lam = (jnp.exp(jnp.sum(lq1_ref[...] * lk1_ref[...], axis=-1, keepdims=True))
           - jnp.exp(jnp.sum(lq2_ref[...] * lk2_ref[...], axis=-1, keepdims=True))
           + LAM_INIT)
    for j in range(n_sub):
        ot = outs[(j, 0)] - lam * outs[(j, 1)]
        ms = jnp.mean(ot * ot, axis=0, keepdims=True)
        yt = (ot * lax.rsqrt(ms + EPS) * sg_ref[...]) * (1.0 - LAM_INIT)
        o_ref[0, j * tqs:(j + 1) * tqs, :] = yt.T.astype(BF16)


def _attention(q, kx, km, vxt, vmt, lams, sg, nb, tq):
    qb, _, nq, _ = q.shape
    s = kx.shape[2]
    grid = (nb, HEADS, nq // tq)
    qmap = (lambda b, h, i: (b, h, i, 0)) if qb == nb else (lambda b, h, i: (0, h, i, 0))
    vec = lambda w: pl.BlockSpec((1, w), lambda b, h, i: (0, 0))
    return pl.pallas_call(
        _attn_kernel,
        grid=grid,
        in_specs=[
            pl.BlockSpec((1, 1, tq, LANES), qmap),
            pl.BlockSpec((1, 1, s, LANES), lambda b, h, i: (b, h, 0, 0)),
            pl.BlockSpec((1, 1, LANES, LANES), lambda b, h, i: (0, h, 0, 0)),
            pl.BlockSpec((1, 1, VALUE_DIM, s), lambda b, h, i: (b, h, 0, 0)),
            pl.BlockSpec((1, 1, VALUE_DIM, LANES), lambda b, h, i: (0, h, 0, 0)),
            vec(HEAD_DIM), vec(HEAD_DIM), vec(HEAD_DIM), vec(HEAD_DIM),
            pl.BlockSpec((VALUE_DIM, 1), lambda b, h, i: (0, 0)),
        ],
        out_specs=pl.BlockSpec((1, tq, VALUE_DIM), lambda b, h, i: (b, i, h)),
        out_shape=jax.ShapeDtypeStruct((nb, nq, ATTN_WIDTH), BF16),
        compiler_params=_cparams(3),
        name="diff_attn",
    )(q, kx, km, vxt, vmt, *lams, sg)


def _pool_kernel(pm_ref, px_ref, pw_ref, ps_ref, ox_ref, om_ref, seq_ref, *, seq_len):
    lp = seq_ref.shape[0]
    s = px_ref.shape[1]
    seq_ref[0:POOL_PAD, :] = jnp.zeros((POOL_PAD, POOL_WIDTH), F32)
    seq_ref[POOL_PAD + seq_len:lp, :] = jnp.zeros((lp - POOL_PAD - seq_len, POOL_WIDTH), F32)
    seq_ref[POOL_PAD:POOL_PAD + N_META, :] = pm_ref[...]
    seq_ref[POOL_PAD + N_META:POOL_PAD + seq_len, :] = px_ref[0]
    t = lax.broadcasted_iota(jnp.int32, (lp, 1), 0) - POOL_PAD
    for g, w in enumerate(POOL_WINDOWS):
        c0, c1 = g * POOL_GROUP_DIM, (g + 1) * POOL_GROUP_DIM
        sg = seq_ref[:, c0:c1]
        win = sg + pltpu.roll(sg, 1, 0)
        span = 1
        while 2 * span < w:
            win = pltpu.roll(win, span, 0) + pltpu.roll(win, lp - span, 0)
            span *= 2
        cnt = jnp.minimum(t + w // 2, seq_len) - jnp.maximum(t - w // 2, 0)
        cnt = jnp.maximum(cnt, 1).astype(F32)
        d = (win / cnt - sg).astype(BF16)
        y = _dot(d, pw_ref[g]) * ps_ref[:, c0:c1]
        om_ref[0, :, c0:c1] = y[POOL_PAD:POOL_PAD + N_META].astype(BF16)
        ox_ref[0, :, c0:c1] = y[POOL_PAD + N_META:POOL_PAD + N_META + s].astype(BF16)


def _pool(p_m, p_x, pool_w, pool_scale):
    nb, s, _ = p_x.shape
    seq_len = s + N_META
    lp = seq_len + 2 * POOL_PAD
    return pl.pallas_call(
        functools.partial(_pool_kernel, seq_len=seq_len),
        grid=(nb,),
        in_specs=[
            pl.BlockSpec((N_META, POOL_WIDTH), lambda b: (0, 0)),
            pl.BlockSpec((1, s, POOL_WIDTH), lambda b: (b, 0, 0)),
            pl.BlockSpec((len(POOL_WINDOWS), POOL_GROUP_DIM, POOL_GROUP_DIM), lambda b: (0, 0, 0)),
            pl.BlockSpec((1, POOL_WIDTH), lambda b: (0, 0)),
        ],
        out_specs=[pl.BlockSpec((1, s, POOL_WIDTH), lambda b: (b, 0, 0)),
                   pl.BlockSpec((1, N_META, POOL_WIDTH), lambda b: (b, 0, 0))],
        out_shape=[jax.ShapeDtypeStruct((nb, s, POOL_WIDTH), BF16),
                   jax.ShapeDtypeStruct((nb, N_META, POOL_WIDTH), BF16)],
        scratch_shapes=[pltpu.VMEM((lp, POOL_WIDTH), F32)],
        compiler_params=_cparams(1),
        name="pool_mixer",
    )(p_m, p_x, pool_w, pool_scale)


def _outproj_kernel(a_ref, p_ref, x_ref, wo_ref, g2_ref, wrh_ref, wrl_ref,
                    h_ref, hn_ref, aff_ref):
    mix = _dot(a_ref[0], wo_ref[0:ATTN_WIDTH, :]) + _dot(p_ref[0], wo_ref[ATTN_WIDTH:, :])
    h = x_ref[0] + mix
    h_ref[0] = h
    ms = jnp.mean(h * h, axis=-1, keepdims=True)
    hn = h * lax.rsqrt(ms + EPS) * g2_ref[...]
    hi = hn.astype(BF16)
    hn_ref[0] = hi
    lo = (hn - hi.astype(F32)).astype(BF16)
    wrh = wrh_ref[...]
    logits = _dot_nt(wrh, hi) + _dot_nt(wrl_ref[...], hi) + _dot_nt(wrh, lo)
    m = jnp.max(logits, axis=0, keepdims=True)
    e = jnp.exp(logits - m)
    aff_ref[0] = e / jnp.sum(e, axis=0, keepdims=True)


def _out_proj(attn, pool, x, w_out, g2, wr_hi, wr_lo, tm):
    nb, n, _ = x.shape
    grid = (nb, n // tm)
    const = lambda shape: pl.BlockSpec(shape, lambda b, i: (0,) * len(shape))
    return pl.pallas_call(
        _outproj_kernel,
        grid=grid,
        in_specs=[
            pl.BlockSpec((1, tm, ATTN_WIDTH), lambda b, i: (b, i, 0)),
            pl.BlockSpec((1, tm, POOL_WIDTH), lambda b, i: (b, i, 0)),
            pl.BlockSpec((1, tm, D_MODEL), lambda b, i: (b, i, 0)),
            const((D_MODEL, D_MODEL)),
            const((1, D_MODEL)),
            const((N_EXPERTS, D_MODEL)),
            const((N_EXPERTS, D_MODEL)),
        ],
        out_specs=[pl.BlockSpec((1, tm, D_MODEL), lambda b, i: (b, i, 0)),
                   pl.BlockSpec((1, tm, D_MODEL), lambda b, i: (b, i, 0)),
                   pl.BlockSpec((1, N_EXPERTS, tm), lambda b, i: (b, 0, i))],
        out_shape=[jax.ShapeDtypeStruct((nb, n, D_MODEL), F32),
                   jax.ShapeDtypeStruct((nb, n, D_MODEL), BF16),
                   jax.ShapeDtypeStruct((nb, N_EXPERTS, n), F32)],
        compiler_params=_cparams(2),
        name="out_proj_router",
    )(attn, pool, x, w_out, g2, wr_hi, wr_lo)


def _select_kernel(aff_ref, u_ref, slot_ref, *, cap):
    aff = aff_ref[...]
    rows = aff.shape[0]

    def body(i, thr):
        cand = thr | jnp.left_shift(jnp.int32(1), 30 - i)
        cand_f = lax.bitcast_convert_type(cand, F32)
        cnt = jnp.sum((aff >= cand_f).astype(jnp.int32), axis=1, keepdims=True)
        return jnp.where(cnt >= cap, cand, thr)

    thr = lax.fori_loop(0, 31, body, jnp.zeros((rows, 1), jnp.int32))
    thr_f = lax.bitcast_convert_type(thr, F32)
    gt = aff > thr_f
    eq = aff == thr_f
    n_gt = jnp.sum(gt.astype(jnp.int32), axis=1, keepdims=True)
    u = u_ref[...]
    rank_eq = _dot(eq.astype(BF16), u)
    sel = gt | (eq & (rank_eq < (cap - n_gt).astype(F32)))
    pos = _dot(sel.astype(BF16), u)
    slot_ref[...] = jnp.where(sel, pos.astype(jnp.int32), -1)


def _select(aff_rows, u, cap, rb):
    r, lpad = aff_rows.shape
    return pl.pallas_call(
        functools.partial(_select_kernel, cap=cap),
        grid=(r // rb,),
        in_specs=[pl.BlockSpec((rb, lpad), lambda i: (i, 0)),
                  pl.BlockSpec((lpad, lpad), lambda i: (0, 0))],
        out_specs=pl.BlockSpec((rb, lpad), lambda i: (i, 0)),
        out_shape=jax.ShapeDtypeStruct((r, lpad), jnp.int32),
        compiler_params=_cparams(1),
        name="expert_choice_select",
    )(aff_rows, u)


def _expert_kernel(slot_ref, aff_ref, hn_ref, wg_ref, wu_ref, wd_ref, ym_ref, yt_ref):
    slot = slot_ref[0]
    s = slot.shape[1]
    rows = SLOT_MAIN + SLOT_TAIL
    hit = slot == lax.broadcasted_iota(jnp.int32, (rows, s), 0)
    xg = _dot(hit.astype(BF16), hn_ref[0]).astype(BF16)
    gate = jnp.sum(jnp.where(hit, aff_ref[0], 0.0), axis=1, keepdims=True)
    g = _dot(xg, wg_ref[0])
    u = _dot(xg, wu_ref[0])
    a = (g * jax.nn.sigmoid(g) * u).astype(BF16)
    y = _dot(a, wd_ref[0]) * gate
    ym_ref[0] = y[0:SLOT_MAIN].astype(BF16)
    yt_ref[0] = y[SLOT_MAIN:rows].astype(BF16)


def _experts(slot3, aff3, hn, wg, wu, wd):
    nb, s, _ = hn.shape
    grid = (N_EXPERTS, nb)
    row = pl.BlockSpec((1, 1, s), lambda e, b: (b * N_EXPERTS + e, 0, 0))
    return pl.pallas_call(
        _expert_kernel,
        grid=grid,
        in_specs=[
            row, row,
            pl.BlockSpec((1, s, D_MODEL), lambda e, b: (b, 0, 0)),
            pl.BlockSpec((1, D_MODEL, D_FF), lambda e, b: (e, 0, 0)),
            pl.BlockSpec((1, D_MODEL, D_FF), lambda e, b: (e, 0, 0)),
            pl.BlockSpec((1, D_FF, D_MODEL), lambda e, b: (e, 0, 0)),
        ],
        out_specs=[pl.BlockSpec((1, SLOT_MAIN, D_MODEL), lambda e, b: (b, e, 0)),
                   pl.BlockSpec((1, SLOT_TAIL, D_MODEL), lambda e, b: (b, e, 0))],
        out_shape=[jax.ShapeDtypeStruct((nb, N_EXPERTS * SLOT_MAIN, D_MODEL), BF16),
                   jax.ShapeDtypeStruct((nb, N_EXPERTS * SLOT_TAIL, D_MODEL), BF16)],
        compiler_params=_cparams(2),
        name="routed_experts",
    )(slot3, aff3, hn, wg, wu, wd)


def _combine_kernel(h_ref, st_ref, ym_ref, yt_ref, o_ref, pt_ref):
    st = st_ref[0]
    tm = st.shape[0]
    n_main = N_EXPERTS * SLOT_MAIN
    lane = lax.broadcasted_iota(jnp.int32, (tm, SLOT_MAIN), 1)
    tail_e = lane // SLOT_TAIL
    tail_c = SLOT_MAIN + lane % SLOT_TAIL
    tail = jnp.zeros((tm, N_EXPERTS * SLOT_TAIL), jnp.bool_)
    for e in range(N_EXPERTS):
        col = st[:, e:e + 1]
        pt_ref[:, e * SLOT_MAIN:(e + 1) * SLOT_MAIN] = (col == lane).astype(BF16)
        tail = tail | ((tail_e == e) & (col == tail_c))
    pt_ref[:, n_main:] = tail.astype(BF16)
    moe = _dot(pt_ref[:, 0:n_main], ym_ref[0]) + _dot(pt_ref[:, n_main:], yt_ref[0])
    o_ref[0] = h_ref[0] + moe


def _combine(h, slot_t, y_main, y_tail, tm):
    nb, s, _ = h.shape
    n_main = N_EXPERTS * SLOT_MAIN
    n_tail = N_EXPERTS * SLOT_TAIL
    assert n_tail == SLOT_MAIN
    return pl.pallas_call(
        _combine_kernel,
        grid=(nb, s // tm),
        in_specs=[
            pl.BlockSpec((1, tm, D_MODEL), lambda b, i: (b, i, 0)),
            pl.BlockSpec((1, tm, N_EXPERTS), lambda b, i: (b, i, 0)),
            pl.BlockSpec((1, n_main, D_MODEL), lambda b, i: (b, 0, 0)),
            pl.BlockSpec((1, n_tail, D_MODEL), lambda b, i: (b, 0, 0)),
        ],
        out_specs=pl.BlockSpec((1, tm, D_MODEL), lambda b, i: (b, i, 0)),
        out_shape=jax.ShapeDtypeStruct((nb, s, D_MODEL), F32),
        scratch_shapes=[pltpu.VMEM((tm, n_main + n_tail), BF16)],
        compiler_params=_cparams(2),
        name="moe_combine",
    )(h, slot_t, y_main, y_tail)


def _rope_tables(pos):
    half = HEAD_DIM // 2
    freqs = ROPE_THETA ** (-jnp.arange(half, dtype=F32) / half)
    ang = pos[:, None] * freqs[None, :]
    cos = jnp.cos(ang)
    sin = jnp.sin(ang)
    cos_t = jnp.concatenate([cos, cos, cos, cos], axis=1)
    sin_t = jnp.concatenate([-sin, sin, -sin, sin], axis=1)
    return cos_t, sin_t


def kernel(x, meta_tokens, ln1_g, w_in, q_norm_g, k_norm_g, lambda_q1, lambda_k1, lambda_q2, lambda_k2,
           subln_g, pool_w, pool_scale, w_out, ln2_g, w_router, w_gate, w_up, w_down):
    nb, s, _ = x.shape
    seq_len = s + N_META
    cap = CAPACITY_FACTOR * seq_len // N_EXPERTS
    assert cap <= SLOT_MAIN + SLOT_TAIL
    l = 0

    w_in_b = w_in[l].astype(BF16)
    w_out_b = w_out[l].astype(BF16)
    wg_b = w_gate[l].astype(BF16)
    wu_b = w_up[l].astype(BF16)
    wd_b = w_down[l].astype(BF16)
    pw_b = pool_w[l].astype(BF16)
    wr_t = w_router[l].T
    wr_hi = wr_t.astype(BF16)
    wr_lo = (wr_t - wr_hi.astype(F32)).astype(BF16)
    g1 = ln1_g[l][None]
    g2 = ln2_g[l][None]
    qg = jnp.tile(q_norm_g[l], 2)[None]
    kg = jnp.tile(k_norm_g[l], 2)[None]
    sg = subln_g[l][:, None]
    ps = pool_scale[l][None]
    lams = (lambda_q1[l][None], lambda_k1[l][None], lambda_q2[l][None], lambda_k2[l][None])
    lane = jnp.arange(LANES)
    gm = (lane[:, None] // HEAD_DIM == lane[None, :] // HEAD_DIM).astype(BF16)
    cos_x, sin_x = _rope_tables(jnp.arange(N_META, seq_len, dtype=F32))
    cos_m, sin_m = _rope_tables(jnp.arange(N_META, dtype=F32))

    tm = _row_tile(s, 512)
    wvt_b = w_in_b[:, 2 * ATTN_WIDTH:3 * ATTN_WIDTH].T
    q_x, k_x, vt_x, p_x = _in_proj(x, g1, w_in_b, wvt_b, cos_x, sin_x, qg, kg, gm, tm)
    q_m, k_m, vt_m, p_m = _in_proj(meta_tokens[None].astype(x.dtype), g1, w_in_b, wvt_b, cos_m, sin_m,
                                   qg, kg, gm, N_META)
    pad_rows = ((0, 0), (0, 0), (0, LANES - N_META), (0, 0))
    q_mp = jnp.pad(q_m, pad_rows)
    k_mp = jnp.pad(k_m, pad_rows)
    vt_mp = jnp.pad(vt_m, ((0, 0), (0, 0), (0, 0), (0, LANES - N_META)))

    attn_x = _attention(q_x, k_x, k_mp, vt_x, vt_mp, lams, sg, nb, _row_tile(s, 1024))
    attn_m = _attention(q_mp, k_x, k_mp, vt_x, vt_mp, lams, sg, nb, LANES)[:, :N_META]

    pool_x, pool_m = _pool(p_m[0], p_x, pw_b, ps)

    h_x, hn_x, aff_x = _out_proj(attn_x, pool_x, x, w_out_b, g2, wr_hi, wr_lo, tm)
    n_mrows = nb * N_META
    meta_rows = jnp.broadcast_to(meta_tokens[None].astype(x.dtype), (nb, N_META, D_MODEL)).reshape(1, n_mrows, D_MODEL)
    _, _, aff_m = _out_proj(attn_m.reshape(1, n_mrows, ATTN_WIDTH), pool_m.reshape(1, n_mrows, POOL_WIDTH),
                            meta_rows, w_out_b, g2, wr_hi, wr_lo, n_mrows)
    aff_m = aff_m.reshape(N_EXPERTS, nb, N_META).transpose(1, 0, 2)

    lpad = -(-seq_len // LANES) * LANES
    aff_seq = jnp.concatenate([aff_m, aff_x], axis=2)
    aff_rows = jnp.pad(aff_seq, ((0, 0), (0, 0), (0, lpad - seq_len)), constant_values=-1.0)
    aff_rows = aff_rows.reshape(nb * N_EXPERTS, lpad)
    idx = jnp.arange(lpad)
    upper = (idx[:, None] < idx[None, :]).astype(BF16)
    slot = _select(aff_rows, upper, cap, min(nb * N_EXPERTS, 128))
    slot_x = slot[:, N_META:seq_len]

    y_main, y_tail = _experts(slot_x.reshape(nb * N_EXPERTS, 1, s), aff_x.reshape(nb * N_EXPERTS, 1, s),
                              hn_x, wg_b, wu_b, wd_b)

    slot_t = slot_x.reshape(nb, N_EXPERTS, s).transpose(0, 2, 1)
    return _combine(h_x, slot_t, y_main, y_tail, tm)
```

```python
import functools
import math

import jax
import jax.numpy as jnp
from jax import lax
from jax.experimental import pallas as pl
from jax.experimental.pallas import tpu as pltpu

F32 = jnp.float32
BF16 = jnp.bfloat16

D_MODEL = 1024
N_META = 16
ATTN_WIDTH = 512
POOL_WIDTH = 512
HEAD_DIM = 64
HEADS = 4
VALUE_DIM = 128
POOL_WINDOWS = (2, 4, 8, 16)
POOL_GROUP_DIM = 128
IN_WIDTH = 3 * ATTN_WIDTH + POOL_WIDTH
N_EXPERTS = 16
CAPACITY_FACTOR = 2
D_FF = 2048
ROPE_THETA = 10000.0
EPS = 1e-6
LAM_INIT = 0.8 - 0.6 * math.exp(-0.3 * 0)

LANES = 128
MXU_DIM = 256
SLOT_ALIGN = 16
SLOT_ROWS = 272
COMBINE_TILE = 256
COMBINE_WINDOW = 128
WIN_HI_LANE = 64
POOL_PAD = 32
SCORE_LOOKAHEAD = 3
VMEM_LIMIT = 56 * 1024 * 1024


def _row_tile(n, target):
    t = min(n, target)
    assert n % t == 0
    return t


def _cparams(n_axes):
    return pltpu.CompilerParams(dimension_semantics=("arbitrary",) * n_axes,
                                vmem_limit_bytes=VMEM_LIMIT)


def _dot(a, b):
    return jnp.dot(a, b, preferred_element_type=F32)


def _dot_nt(a, b):
    return lax.dot_general(a, b, (((1,), (1,)), ((), ())), preferred_element_type=F32)


def _inproj_kernel(x_ref, g1_ref, w_ref, wvt_ref, cos_ref, sin_ref, qg_ref, kg_ref, gm_ref,
                   q_ref, k_ref, vt_ref, p_ref):
    x = x_ref[0]
    ms = jnp.mean(x * x, axis=-1, keepdims=True)
    xn = (x * lax.rsqrt(ms + EPS) * g1_ref[...]).astype(BF16)
    cos = cos_ref[...]
    sin = sin_ref[...]
    gm = gm_ref[...]
    tm = x.shape[0]
    lane = lax.broadcasted_iota(jnp.int32, (tm, LANES), 1)
    low_half = (lane & 32) == 0

    def norm_rope(t, ss, g, scale):
        tn = t * lax.rsqrt(ss * (1.0 / HEAD_DIM) + EPS) * g
        sw = jnp.where(low_half, pltpu.roll(tn, LANES - 32, 1), pltpu.roll(tn, 32, 1))
        r = tn * cos + sw * sin
        if scale != 1.0:
            r = r * scale
        return r.astype(BF16)

    pq = _dot(xn, w_ref[:, 0:ATTN_WIDTH])
    pk = _dot(xn, w_ref[:, ATTN_WIDTH:2 * ATTN_WIDTH])
    for h in range(HEADS):
        tq = pq[:, h * LANES:(h + 1) * LANES]
        tk = pk[:, h * LANES:(h + 1) * LANES]
        sq = jnp.concatenate([tq * tq, tk * tk], axis=1)
        hi = sq.astype(BF16)
        lo = (sq - hi.astype(F32)).astype(BF16)
        ss = _dot(hi, gm) + _dot(lo, gm)
        q_ref[0, h] = norm_rope(tq, ss[:, 0:LANES], qg_ref[...], HEAD_DIM ** -0.5 * math.log2(math.e))
        k_ref[0, h] = norm_rope(tk, ss[:, LANES:2 * LANES], kg_ref[...], 1.0)
    pvt = _dot_nt(wvt_ref[...], xn)
    for h in range(HEADS):
        vt_ref[0, h] = pvt[h * VALUE_DIM:(h + 1) * VALUE_DIM, :].astype(BF16)
    p_ref[0] = _dot(xn, w_ref[:, 3 * ATTN_WIDTH:IN_WIDTH])


def _in_proj(x, g1, w_in, wvt, cos, sin, qg, kg, gm, tm):
    nb, n, _ = x.shape
    grid = (nb, n // tm)
    hd = jax.ShapeDtypeStruct((nb, HEADS, n, LANES), BF16)
    hdt = jax.ShapeDtypeStruct((nb, HEADS, VALUE_DIM, n), BF16)
    head_spec = pl.BlockSpec((1, HEADS, tm, LANES), lambda b, i: (b, 0, i, 0))
    headt_spec = pl.BlockSpec((1, HEADS, VALUE_DIM, tm), lambda b, i: (b, 0, 0, i))
    const = lambda shape: pl.BlockSpec(shape, lambda b, i: (0,) * len(shape))
    return pl.pallas_call(
        _inproj_kernel,
        grid=grid,
        in_specs=[
            pl.BlockSpec((1, tm, D_MODEL), lambda b, i: (b, i, 0)),
            const((1, D_MODEL)),
            const((D_MODEL, IN_WIDTH)),
            const((ATTN_WIDTH, D_MODEL)),
            pl.BlockSpec((tm, LANES), lambda b, i: (i, 0)),
            pl.BlockSpec((tm, LANES), lambda b, i: (i, 0)),
            const((1, LANES)),
            const((1, LANES)),
            const((MXU_DIM, MXU_DIM)),
        ],
        out_specs=[head_spec, head_spec, headt_spec,
                   pl.BlockSpec((1, tm, POOL_WIDTH), lambda b, i: (b, i, 0))],
        out_shape=[hd, hd, hdt, jax.ShapeDtypeStruct((nb, n, POOL_WIDTH), F32)],
        compiler_params=_cparams(2),
        name="in_proj",
    )(x, g1, w_in, wvt, cos, sin, qg, kg, gm)


def _attn_kernel(q_ref, kx_ref, km_ref, vxt_ref, vmt_ref, lq1_ref, lk1_ref, lq2_ref, lk2_ref,
                 sg_ref, o_ref):
    q = q_ref[0, 0]
    tq = q.shape[0]
    lane = lax.broadcasted_iota(jnp.int32, (tq, LANES), 1)
    zero = jnp.zeros_like(q)
    q_maps = (jnp.where(lane < HEAD_DIM, q, zero), jnp.where(lane >= HEAD_DIM, q, zero))
    kx = kx_ref[0, 0]
    km = km_ref[0, 0]
    vxt = vxt_ref[0, 0]
    vmt = vmt_ref[0, 0]
    tqs = MXU_DIM if tq % MXU_DIM == 0 else tq
    n_sub = tq // tqs
    meta_valid = lax.broadcasted_iota(jnp.int32, (LANES, tqs), 0) < N_META

    def scores(qq):
        sx = _dot_nt(kx, qq)
        sm = jnp.where(meta_valid, _dot_nt(km, qq), -1e30)
        m = jnp.maximum(jnp.max(sx, axis=0, keepdims=True), jnp.max(sm, axis=0, keepdims=True))
        return sx, sm, m

    def weighted_values(sx, sm, m):
        px = jnp.exp2(sx - m)
        pm = jnp.exp2(sm - m)
        l = jnp.sum(px, axis=0, keepdims=True) + jnp.sum(pm, axis=0, keepdims=True)
        ot = _dot(vxt, px.astype(BF16)) + _dot(vmt, pm.astype(BF16))
        return ot / l

    units = [(j, mp) for j in range(n_sub) for mp in range(2)]
    outs = {}
    pending = []
    for j, mp in units:
        pending.append(((j, mp), scores(q_maps[mp][j * tqs:(j + 1) * tqs])))
        if len(pending) > SCORE_LOOKAHEAD:
            key, args = pending.pop(0)
            outs[key] = weighted_values(*args)
    for key, args in pending:
        outs[key] = weighted_values(*args)

    lam = (jnp.exp(jnp.sum(lq1_ref[...] * lk1_ref[...], axis=-1, keepdims=True))
           - jnp.exp(jnp.sum(lq2_ref[...] * lk2_ref[...], axis=-1, keepdims=True))
           + LAM_INIT)
    for j in range(n_sub):
        ot = outs[(j, 0)] - lam * outs[(j, 1)]
        ms = jnp.mean(ot * ot, axis=0, keepdims=True)
        yt = (ot * lax.rsqrt(ms + EPS) * sg_ref[...]) * (1.0 - LAM_INIT)
        o_ref[0, j * tqs:(j + 1) * tqs, :] = yt.T.astype(BF16)


def _attention(q, kx, km, vxt, vmt, lams, sg, nb, tq):
    qb, _, nq, _ = q.shape
    s = kx.shape[2]
    grid = (nb, HEADS, nq // tq)
    qmap = (lambda b, h, i: (b, h, i, 0)) if qb == nb else (lambda b, h, i: (0, h, i, 0))
    vec = lambda w: pl.BlockSpec((1, w), lambda b, h, i: (0, 0))
    return pl.pallas_call(
        _attn_kernel,
        grid=grid,
        in_specs=[
            pl.BlockSpec((1, 1, tq, LANES), qmap),
            pl.BlockSpec((1, 1, s, LANES), lambda b, h, i: (b, h, 0, 0)),
            pl.BlockSpec((1, 1, LANES, LANES), lambda b, h, i: (0, h, 0, 0)),
            pl.BlockSpec((1, 1, VALUE_DIM, s), lambda b, h, i: (b, h, 0, 0)),
            pl.BlockSpec((1, 1, VALUE_DIM, LANES), lambda b, h, i: (0, h, 0, 0)),
            vec(HEAD_DIM), vec(HEAD_DIM), vec(HEAD_DIM), vec(HEAD_DIM),
            pl.BlockSpec((VALUE_DIM, 1), lambda b, h, i: (0, 0)),
        ],
        out_specs=pl.BlockSpec((1, tq, VALUE_DIM), lambda b, h, i: (b, i, h)),
        out_shape=jax.ShapeDtypeStruct((nb, nq, ATTN_WIDTH), BF16),
        compiler_params=_cparams(3),
        name="diff_attn",
    )(q, kx, km, vxt, vmt, *lams, sg)


def _pool_kernel(pm_ref, px_ref, pw_ref, ps_ref, ox_ref, om_ref, seq_ref, *, seq_len):
    lp = seq_ref.shape[0]
    s = px_ref.shape[1]
    seq_ref[0:POOL_PAD, :] = jnp.zeros((POOL_PAD, POOL_WIDTH), F32)
    seq_ref[POOL_PAD + seq_len:lp, :] = jnp.zeros((lp - POOL_PAD - seq_len, POOL_WIDTH), F32)
    seq_ref[POOL_PAD:POOL_PAD + N_META, :] = pm_ref[...]
    seq_ref[POOL_PAD + N_META:POOL_PAD + seq_len, :] = px_ref[0]
    t = lax.broadcasted_iota(jnp.int32, (lp, 1), 0) - POOL_PAD
    for g, w in enumerate(POOL_WINDOWS):
        c0, c1 = g * POOL_GROUP_DIM, (g + 1) * POOL_GROUP_DIM
        sg = seq_ref[:, c0:c1]
        win = sg + pltpu.roll(sg, 1, 0)
        span = 1
        while 2 * span < w:
            win = pltpu.roll(win, span, 0) + pltpu.roll(win, lp - span, 0)
            span *= 2
        cnt = jnp.minimum(t + w // 2, seq_len) - jnp.maximum(t - w // 2, 0)
        cnt = jnp.maximum(cnt, 1).astype(F32)
        d = (win / cnt - sg).astype(BF16)
        y = _dot(d, pw_ref[g]) * ps_ref[:, c0:c1]
        om_ref[0, :, c0:c1] = y[POOL_PAD:POOL_PAD + N_META].astype(BF16)
        ox_ref[0, :, c0:c1] = y[POOL_PAD + N_META:POOL_PAD + N_META + s].astype(BF16)


def _pool(p_m, p_x, pool_w, pool_scale):
    nb, s, _ = p_x.shape
    seq_len = s + N_META
    lp = seq_len + 2 * POOL_PAD
    return pl.pallas_call(
        functools.partial(_pool_kernel, seq_len=seq_len),
        grid=(nb,),
        in_specs=[
            pl.BlockSpec((N_META, POOL_WIDTH), lambda b: (0, 0)),
            pl.BlockSpec((1, s, POOL_WIDTH), lambda b: (b, 0, 0)),
            pl.BlockSpec((len(POOL_WINDOWS), POOL_GROUP_DIM, POOL_GROUP_DIM), lambda b: (0, 0, 0)),
            pl.BlockSpec((1, POOL_WIDTH), lambda b: (0, 0)),
        ],
        out_specs=[pl.BlockSpec((1, s, POOL_WIDTH), lambda b: (b, 0, 0)),
                   pl.BlockSpec((1, N_META, POOL_WIDTH), lambda b: (b, 0, 0))],
        out_shape=[jax.ShapeDtypeStruct((nb, s, POOL_WIDTH), BF16),
                   jax.ShapeDtypeStruct((nb, N_META, POOL_WIDTH), BF16)],
        scratch_shapes=[pltpu.VMEM((lp, POOL_WIDTH), F32)],
        compiler_params=_cparams(1),
        name="pool_mixer",
    )(p_m, p_x, pool_w, pool_scale)


def _outproj_kernel(a_ref, p_ref, x_ref, wo_ref, g2_ref, wrh_ref, wrl_ref,
                    h_ref, hn_ref, aff_ref):
    mix = _dot(a_ref[0], wo_ref[0:ATTN_WIDTH, :]) + _dot(p_ref[0], wo_ref[ATTN_WIDTH:, :])
    h = x_ref[0] + mix
    h_ref[0] = h
    ms = jnp.mean(h * h, axis=-1, keepdims=True)
    hn = h * lax.rsqrt(ms + EPS) * g2_ref[...]
    hi = hn.astype(BF16)
    hn_ref[0] = hi
    lo = (hn - hi.astype(F32)).astype(BF16)
    wrh = wrh_ref[...]
    logits = _dot_nt(wrh, hi) + _dot_nt(wrl_ref[...], hi) + _dot_nt(wrh, lo)
    m = jnp.max(logits, axis=0, keepdims=True)
    e = jnp.exp(logits - m)
    aff_ref[0] = e / jnp.sum(e, axis=0, keepdims=True)


def _out_proj(attn, pool, x, w_out, g2, wr_hi, wr_lo, tm):
    nb, n, _ = x.shape
    grid = (nb, n // tm)
    const = lambda shape: pl.BlockSpec(shape, lambda b, i: (0,) * len(shape))
    return pl.pallas_call(
        _outproj_kernel,
        grid=grid,
        in_specs=[
            pl.BlockSpec((1, tm, ATTN_WIDTH), lambda b, i: (b, i, 0)),
            pl.BlockSpec((1, tm, POOL_WIDTH), lambda b, i: (b, i, 0)),
            pl.BlockSpec((1, tm, D_MODEL), lambda b, i: (b, i, 0)),
            const((D_MODEL, D_MODEL)),
            const((1, D_MODEL)),
            const((N_EXPERTS, D_MODEL)),
            const((N_EXPERTS, D_MODEL)),
        ],
        out_specs=[pl.BlockSpec((1, tm, D_MODEL), lambda b, i: (b, i, 0)),
                   pl.BlockSpec((1, tm, D_MODEL), lambda b, i: (b, i, 0)),
                   pl.BlockSpec((1, N_EXPERTS, tm), lambda b, i: (b, 0, i))],
        out_shape=[jax.ShapeDtypeStruct((nb, n, D_MODEL), F32),
                   jax.ShapeDtypeStruct((nb, n, D_MODEL), BF16),
                   jax.ShapeDtypeStruct((nb, N_EXPERTS, n), F32)],
        compiler_params=_cparams(2),
        name="out_proj_router",
    )(attn, pool, x, w_out, g2, wr_hi, wr_lo)


def _select_kernel(aff_ref, u_ref, slot_ref, win_ref, *, cap, n_tok):
    aff = aff_ref[...]
    rows, lpad = aff.shape

    def body(i, thr):
        cand = thr | jnp.left_shift(jnp.int32(1), 30 - i)
        cand_f = lax.bitcast_convert_type(cand, F32)
        cnt = jnp.sum((aff >= cand_f).astype(jnp.int32), axis=1, keepdims=True)
        return jnp.where(cnt >= cap, cand, thr)

    thr = lax.fori_loop(0, 31, body, jnp.zeros((rows, 1), jnp.int32))
    thr_f = lax.bitcast_convert_type(thr, F32)
    gt = aff > thr_f
    eq = aff == thr_f
    n_gt = jnp.sum(gt.astype(jnp.int32), axis=1, keepdims=True)
    u = u_ref[...]
    rank_eq = _dot(eq.astype(BF16), u)
    sel = gt | (eq & (rank_eq < (cap - n_gt).astype(F32)))
    pos = _dot(sel.astype(BF16), u)
    slot = jnp.where(sel, pos.astype(jnp.int32), -1)
    slot_ref[...] = slot

    lane = lax.broadcasted_iota(jnp.int32, (rows, lpad), 1)
    out_lane = lax.broadcasted_iota(jnp.int32, (rows, LANES), 1)
    win = jnp.zeros((rows, LANES), jnp.int32)
    n_tiles = n_tok // COMBINE_TILE
    big = jnp.int32(1 << 20)
    for j in range(n_tiles):
        t0 = N_META + j * COMBINE_TILE
        inside = (lane >= t0) & (lane < t0 + COMBINE_TILE) & sel
        lo = jnp.min(jnp.where(inside, slot, big), axis=1, keepdims=True)
        hi = jnp.max(jnp.where(inside, slot, -1), axis=1, keepdims=True)
        first = jnp.where(lo == big, 0, (lo // SLOT_ALIGN) * SLOT_ALIGN)
        win = jnp.where(out_lane == j, first, win)
        win = jnp.where(out_lane == WIN_HI_LANE + j, hi, win)
    win_ref[...] = win


def _select(aff_rows, u, cap, rb, n_tok):
    r, lpad = aff_rows.shape
    assert n_tok // COMBINE_TILE <= WIN_HI_LANE
    return pl.pallas_call(
        functools.partial(_select_kernel, cap=cap, n_tok=n_tok),
        grid=(r // rb,),
        in_specs=[pl.BlockSpec((rb, lpad), lambda i: (i, 0)),
                  pl.BlockSpec((lpad, lpad), lambda i: (0, 0))],
        out_specs=[pl.BlockSpec((rb, lpad), lambda i: (i, 0)),
                   pl.BlockSpec((rb, LANES), lambda i: (i, 0))],
        out_shape=[jax.ShapeDtypeStruct((r, lpad), jnp.int32),
                   jax.ShapeDtypeStruct((r, LANES), jnp.int32)],
        compiler_params=_cparams(1),
        name="expert_choice_select",
    )(aff_rows, u)


def _expert_kernel(slot_ref, aff_ref, hn_ref, wg32_ref, wu32_ref, wd32_ref, y_ref, wg_s, wu_s, wd_s):
    p = pl.program_id(0)
    b = pl.program_id(1)
    fill = p % 2

    @pl.when(p < N_EXPERTS)
    def _():
        rg = wg32_ref.shape[1]
        rd = wd32_ref.shape[1]
        wg_s[fill, pl.ds(pl.multiple_of(b * rg, rg), rg), :] = wg32_ref[0].astype(BF16)
        wu_s[fill, pl.ds(pl.multiple_of(b * rg, rg), rg), :] = wu32_ref[0].astype(BF16)
        wd_s[fill, pl.ds(pl.multiple_of(b * rd, rd), rd), :] = wd32_ref[0].astype(BF16)

    @pl.when(p == 0)
    def _():
        y_ref[0] = jnp.zeros(y_ref.shape[1:], BF16)

    @pl.when(p > 0)
    def _():
        use = 1 - fill
        slot = slot_ref[0]
        s = slot.shape[1]
        hit = slot == lax.broadcasted_iota(jnp.int32, (SLOT_ROWS, s), 0)
        xg = _dot(hit.astype(BF16), hn_ref[0]).astype(BF16)
        gate = jnp.sum(jnp.where(hit, aff_ref[0], 0.0), axis=1, keepdims=True)
        g = _dot(xg, wg_s[use])
        u = _dot(xg, wu_s[use])
        a = (g * jax.nn.sigmoid(g) * u).astype(BF16)
        y_ref[0] = (_dot(a, wd_s[use]) * gate).astype(BF16)


def _experts(slot3, aff3, hn, wg, wu, wd):
    nb, s, _ = hn.shape
    assert D_MODEL % nb == 0 and (D_MODEL // nb) % 16 == 0
    rg = D_MODEL // nb
    rd = D_FF // nb
    grid = (N_EXPERTS + 1, nb)
    last = N_EXPERTS - 1
    row = pl.BlockSpec((1, 1, s), lambda p, b: (b * N_EXPERTS + jnp.maximum(p - 1, 0), 0, 0))
    wmap = lambda p, b: (jnp.minimum(p, last), jnp.where(p <= last, b, nb - 1), 0)
    return pl.pallas_call(
        _expert_kernel,
        grid=grid,
        in_specs=[
            row, row,
            pl.BlockSpec((1, s, D_MODEL), lambda p, b: (b, 0, 0)),
            pl.BlockSpec((1, rg, D_FF), wmap),
            pl.BlockSpec((1, rg, D_FF), wmap),
            pl.BlockSpec((1, rd, D_MODEL), wmap),
        ],
        out_specs=pl.BlockSpec((1, SLOT_ROWS, D_MODEL), lambda p, b: (b, p, 0)),
        out_shape=jax.ShapeDtypeStruct((nb, (N_EXPERTS + 1) * SLOT_ROWS, D_MODEL), BF16),
        scratch_shapes=[pltpu.VMEM((2, D_MODEL, D_FF), BF16),
                        pltpu.VMEM((2, D_MODEL, D_FF), BF16),
                        pltpu.VMEM((2, D_FF, D_MODEL), BF16)],
        compiler_params=_cparams(2),
        name="routed_experts",
    )(slot3, aff3, hn, wg, wu, wd)


def _combine_kernel(win_ref, h_ref, st_ref, y_ref, o_ref):
    j = pl.program_id(1)
    st = st_ref[0]
    tm = st.shape[0]
    lane = lax.broadcasted_iota(jnp.int32, (tm, COMBINE_WINDOW), 1)
    firsts = [win_ref[0, e, j] for e in range(N_EXPERTS)]
    n_pass = jnp.int32(1)
    for e in range(N_EXPERTS):
        span = win_ref[0, e, WIN_HI_LANE + j] - firsts[e]
        n_pass = jnp.maximum(n_pass, span // COMBINE_WINDOW + 1)

    per_dot = MXU_DIM // COMBINE_WINDOW

    def one_pass(k, acc):
        for e0 in range(0, N_EXPERTS, per_dot):
            hits, rows = [], []
            for e in range(e0, e0 + per_dot):
                fresh = firsts[e] + k * COMBINE_WINDOW
                start = pl.multiple_of(jnp.minimum(fresh, SLOT_ROWS - COMBINE_WINDOW), SLOT_ALIGN)
                rows.append(y_ref[0, pl.ds(pl.multiple_of((e + 1) * SLOT_ROWS + start, SLOT_ALIGN),
                                           COMBINE_WINDOW), :])
                col = st[:, e:e + 1]
                hits.append(((col == start + lane) & (col >= fresh)).astype(BF16))
            acc = acc + _dot(jnp.concatenate(hits, axis=1), jnp.concatenate(rows, axis=0))
        return acc

    o_ref[0] = lax.fori_loop(0, n_pass, one_pass, h_ref[0])


def _combine(h, slot_t, win, y):
    nb, s, _ = h.shape
    tm = COMBINE_TILE
    return pl.pallas_call(
        _combine_kernel,
        grid=(nb, s // tm),
        in_specs=[
            pl.BlockSpec((1, N_EXPERTS, LANES), lambda b, i: (b, 0, 0), memory_space=pltpu.SMEM),
            pl.BlockSpec((1, tm, D_MODEL), lambda b, i: (b, i, 0)),
            pl.BlockSpec((1, tm, N_EXPERTS), lambda b, i: (b, i, 0)),
            pl.BlockSpec((1, (N_EXPERTS + 1) * SLOT_ROWS, D_MODEL), lambda b, i: (b, 0, 0)),
        ],
        out_specs=pl.BlockSpec((1, tm, D_MODEL), lambda b, i: (b, i, 0)),
        out_shape=jax.ShapeDtypeStruct((nb, s, D_MODEL), F32),
        compiler_params=_cparams(2),
        name="moe_combine",
    )(win, h, slot_t, y)


def _rope_tables(pos):
    half = HEAD_DIM // 2
    freqs = ROPE_THETA ** (-jnp.arange(half, dtype=F32) / half)
    ang = pos[:, None] * freqs[None, :]
    cos = jnp.cos(ang)
    sin = jnp.sin(ang)
    cos_t = jnp.concatenate([cos, cos, cos, cos], axis=1)
    sin_t = jnp.concatenate([-sin, sin, -sin, sin], axis=1)
    return cos_t, sin_t


def kernel(x, meta_tokens, ln1_g, w_in, q_norm_g, k_norm_g, lambda_q1, lambda_k1, lambda_q2, lambda_k2,
           subln_g, pool_w, pool_scale, w_out, ln2_g, w_router, w_gate, w_up, w_down):
    nb, s, _ = x.shape
    seq_len = s + N_META
    cap = CAPACITY_FACTOR * seq_len // N_EXPERTS
    assert cap <= SLOT_ROWS and s % COMBINE_TILE == 0
    l = 0

    w_in_b = w_in[l].astype(BF16)
    w_out_b = w_out[l].astype(BF16)
    pw_b = pool_w[l].astype(BF16)
    wr_t = w_router[l].T
    wr_hi = wr_t.astype(BF16)
    wr_lo = (wr_t - wr_hi.astype(F32)).astype(BF16)
    g1 = ln1_g[l][None]
    g2 = ln2_g[l][None]
    qg = jnp.tile(q_norm_g[l], 2)[None]
    kg = jnp.tile(k_norm_g[l], 2)[None]
    sg = subln_g[l][:, None]
    ps = pool_scale[l][None]
    lams = (lambda_q1[l][None], lambda_k1[l][None], lambda_q2[l][None], lambda_k2[l][None])
    lane = jnp.arange(MXU_DIM)
    gm = (lane[:, None] // HEAD_DIM == lane[None, :] // HEAD_DIM).astype(BF16)
    cos_x, sin_x = _rope_tables(jnp.arange(N_META, seq_len, dtype=F32))
    cos_m, sin_m = _rope_tables(jnp.arange(N_META, dtype=F32))

    tm = _row_tile(s, 512)
    wvt_b = w_in_b[:, 2 * ATTN_WIDTH:3 * ATTN_WIDTH].T
    q_x, k_x, vt_x, p_x = _in_proj(x, g1, w_in_b, wvt_b, cos_x, sin_x, qg, kg, gm, tm)
    q_m, k_m, vt_m, p_m = _in_proj(meta_tokens[None].astype(x.dtype), g1, w_in_b, wvt_b, cos_m, sin_m,
                                   qg, kg, gm, N_META)
    pad_rows = ((0, 0), (0, 0), (0, LANES - N_META), (0, 0))
    q_mp = jnp.pad(q_m, pad_rows)
    k_mp = jnp.pad(k_m, pad_rows)
    vt_mp = jnp.pad(vt_m, ((0, 0), (0, 0), (0, 0), (0, LANES - N_META)))

    attn_x = _attention(q_x, k_x, k_mp, vt_x, vt_mp, lams, sg, nb, _row_tile(s, 1024))
    attn_m = _attention(q_mp, k_x, k_mp, vt_x, vt_mp, lams, sg, nb, LANES)[:, :N_META]

    pool_x, pool_m = _pool(p_m[0], p_x, pw_b, ps)

    h_x, hn_x, aff_x = _out_proj(attn_x, pool_x, x, w_out_b, g2, wr_hi, wr_lo, tm)
    n_mrows = nb * N_META
    meta_rows = jnp.broadcast_to(meta_tokens[None].astype(x.dtype), (nb, N_META, D_MODEL)).reshape(1, n_mrows, D_MODEL)
    _, _, aff_m = _out_proj(attn_m.reshape(1, n_mrows, ATTN_WIDTH), pool_m.reshape(1, n_mrows, POOL_WIDTH),
                            meta_rows, w_out_b, g2, wr_hi, wr_lo, n_mrows)
    aff_m = aff_m.reshape(N_EXPERTS, nb, N_META).transpose(1, 0, 2)

    lpad = -(-seq_len // LANES) * LANES
    aff_seq = jnp.concatenate([aff_m, aff_x], axis=2)
    aff_rows = jnp.pad(aff_seq, ((0, 0), (0, 0), (0, lpad - seq_len)), constant_values=-1.0)
    aff_rows = aff_rows.reshape(nb * N_EXPERTS, lpad)
    idx = jnp.arange(lpad)
    upper = (idx[:, None] < idx[None, :]).astype(BF16)
    slot, win = _select(aff_rows, upper, cap, min(nb * N_EXPERTS, 128), s)
    slot_x = slot[:, N_META:seq_len]

    y = _experts(slot_x.reshape(nb * N_EXPERTS, 1, s), aff_x.reshape(nb * N_EXPERTS, 1, s),
                 hn_x, w_gate[l], w_up[l], w_down[l])

    slot_t = slot_x.reshape(nb, N_EXPERTS, s).transpose(0, 2, 1)
    return _combine(h_x, slot_t, win.reshape(nb, N_EXPERTS, LANES), y)
```

```python
import functools
import math

import jax
import jax.numpy as jnp
from jax import lax
from jax.experimental import pallas as pl
from jax.experimental.pallas import tpu as pltpu

F32 = jnp.float32
BF16 = jnp.bfloat16

D_MODEL = 1024
N_META = 16
ATTN_WIDTH = 512
POOL_WIDTH = 512
HEAD_DIM = 64
HEADS = 4
VALUE_DIM = 128
POOL_WINDOWS = (2, 4, 8, 16)
POOL_GROUP_DIM = 128
IN_WIDTH = 3 * ATTN_WIDTH + POOL_WIDTH
N_EXPERTS = 16
CAPACITY_FACTOR = 2
D_FF = 2048
ROPE_THETA = 10000.0
EPS = 1e-6
LAM_INIT = 0.8 - 0.6 * math.exp(-0.3 * 0)

LANES = 128
MXU_DIM = 256
WORD_ROWS = D_MODEL // LANES
SLOT_ALIGN = 16
SLOT_ROWS = 272
GATHER_STRIDE = SLOT_ROWS + 1
GATHER_ROWS = -(-WORD_ROWS * GATHER_STRIDE // 8) * 8
SLOT_LANES = 384
COMBINE_TILE = 256
COMBINE_WINDOW = 128
WIN_HI_LANE = 64
POOL_PAD = 32
SCORE_LOOKAHEAD = 3
VMEM_LIMIT = 56 * 1024 * 1024


def _row_tile(n, target):
    t = min(n, target)
    assert n % t == 0
    return t


def _cparams(n_axes):
    return pltpu.CompilerParams(dimension_semantics=("arbitrary",) * n_axes,
                                vmem_limit_bytes=VMEM_LIMIT)


def _dot(a, b):
    return jnp.dot(a, b, preferred_element_type=F32)


def _dot_nt(a, b):
    return lax.dot_general(a, b, (((1,), (1,)), ((), ())), preferred_element_type=F32)


def _inproj_kernel(x_ref, g1_ref, w_ref, wvt_ref, cos_ref, sin_ref, qg_ref, kg_ref, gm_ref,
                   q_ref, k_ref, vt_ref, p_ref):
    x = x_ref[0]
    ms = jnp.mean(x * x, axis=-1, keepdims=True)
    xn = (x * lax.rsqrt(ms + EPS) * g1_ref[...]).astype(BF16)
    cos = cos_ref[...]
    sin = sin_ref[...]
    gm = gm_ref[...]
    tm = x.shape[0]
    lane = lax.broadcasted_iota(jnp.int32, (tm, LANES), 1)
    low_half = (lane & 32) == 0

    def norm_rope(t, ss, g, scale):
        tn = t * lax.rsqrt(ss * (1.0 / HEAD_DIM) + EPS) * g
        sw = jnp.where(low_half, pltpu.roll(tn, LANES - 32, 1), pltpu.roll(tn, 32, 1))
        r = tn * cos + sw * sin
        if scale != 1.0:
            r = r * scale
        return r.astype(BF16)

    pq = _dot(xn, w_ref[:, 0:ATTN_WIDTH])
    pk = _dot(xn, w_ref[:, ATTN_WIDTH:2 * ATTN_WIDTH])
    for h in range(HEADS):
        tq = pq[:, h * LANES:(h + 1) * LANES]
        tk = pk[:, h * LANES:(h + 1) * LANES]
        sq = jnp.concatenate([tq * tq, tk * tk], axis=1)
        hi = sq.astype(BF16)
        lo = (sq - hi.astype(F32)).astype(BF16)
        ss = _dot(hi, gm) + _dot(lo, gm)
        q_ref[0, h] = norm_rope(tq, ss[:, 0:LANES], qg_ref[...], HEAD_DIM ** -0.5 * math.log2(math.e))
        k_ref[0, h] = norm_rope(tk, ss[:, LANES:2 * LANES], kg_ref[...], 1.0)
    pvt = _dot_nt(wvt_ref[...], xn)
    for h in range(HEADS):
        vt_ref[0, h] = pvt[h * VALUE_DIM:(h + 1) * VALUE_DIM, :].astype(BF16)
    p_ref[0] = _dot(xn, w_ref[:, 3 * ATTN_WIDTH:IN_WIDTH])


def _in_proj(x, g1, w_in, wvt, cos, sin, qg, kg, gm, tm):
    nb, n, _ = x.shape
    grid = (nb, n // tm)
    hd = jax.ShapeDtypeStruct((nb, HEADS, n, LANES), BF16)
    hdt = jax.ShapeDtypeStruct((nb, HEADS, VALUE_DIM, n), BF16)
    head_spec = pl.BlockSpec((1, HEADS, tm, LANES), lambda b, i: (b, 0, i, 0))
    headt_spec = pl.BlockSpec((1, HEADS, VALUE_DIM, tm), lambda b, i: (b, 0, 0, i))
    const = lambda shape: pl.BlockSpec(shape, lambda b, i: (0,) * len(shape))
    return pl.pallas_call(
        _inproj_kernel,
        grid=grid,
        in_specs=[
            pl.BlockSpec((1, tm, D_MODEL), lambda b, i: (b, i, 0)),
            const((1, D_MODEL)),
            const((D_MODEL, IN_WIDTH)),
            const((ATTN_WIDTH, D_MODEL)),
            pl.BlockSpec((tm, LANES), lambda b, i: (i, 0)),
            pl.BlockSpec((tm, LANES), lambda b, i: (i, 0)),
            const((1, LANES)),
            const((1, LANES)),
            const((MXU_DIM, MXU_DIM)),
        ],
        out_specs=[head_spec, head_spec, headt_spec,
                   pl.BlockSpec((1, tm, POOL_WIDTH), lambda b, i: (b, i, 0))],
        out_shape=[hd, hd, hdt, jax.ShapeDtypeStruct((nb, n, POOL_WIDTH), F32)],
        compiler_params=_cparams(2),
        name="in_proj",
    )(x, g1, w_in, wvt, cos, sin, qg, kg, gm)


def _attn_kernel(q_ref, kx_ref, km_ref, vxt_ref, vmt_ref, lq1_ref, lk1_ref, lq2_ref, lk2_ref,
                 sg_ref, o_ref):
    q = q_ref[0, 0]
    tq = q.shape[0]
    lane = lax.broadcasted_iota(jnp.int32, (tq, LANES), 1)
    zero = jnp.zeros_like(q)
    q_maps = (jnp.where(lane < HEAD_DIM, q, zero), jnp.where(lane >= HEAD_DIM, q, zero))
    kx = kx_ref[0, 0]
    km = km_ref[0, 0]
    vxt = vxt_ref[0, 0]
    vmt = vmt_ref[0, 0]
    tqs = MXU_DIM if tq % MXU_DIM == 0 else tq
    n_sub = tq // tqs
    meta_valid = lax.broadcasted_iota(jnp.int32, (LANES, tqs), 0) < N_META

    def scores(qq):
        sx = _dot_nt(kx, qq)
        sm = jnp.where(meta_valid, _dot_nt(km, qq), -1e30)
        m = jnp.maximum(jnp.max(sx, axis=0, keepdims=True), jnp.max(sm, axis=0, keepdims=True))
        return sx, sm, m

    def weighted_values(sx, sm, m):
        px = jnp.exp2(sx - m)
        pm = jnp.exp2(sm - m)
        l = jnp.sum(px, axis=0, keepdims=True) + jnp.sum(pm, axis=0, keepdims=True)
        ot = _dot(vxt, px.astype(BF16)) + _dot(vmt, pm.astype(BF16))
        return ot / l

    units = [(j, mp) for j in range(n_sub) for mp in range(2)]
    outs = {}
    pending = []
    for j, mp in units:
        pending.append(((j, mp), scores(q_maps[mp][j * tqs:(j + 1) * tqs])))
        if len(pending) > SCORE_LOOKAHEAD:
            key, args = pending.pop(0)
            outs[key] = weighted_values(*args)
    for key, args in pending:
        outs[key] = weighted_values(*args)

    lam = (jnp.exp(jnp.sum(lq1_ref[...] * lk1_ref[...], axis=-1, keepdims=True))
           - jnp.exp(jnp.sum(lq2_ref[...] * lk2_ref[...], axis=-1, keepdims=True))
           + LAM_INIT)
    for j in range(n_sub):
        ot = outs[(j, 0)] - lam * outs[(j, 1)]
        ms = jnp.mean(ot * ot, axis=0, keepdims=True)
        yt = (ot * lax.rsqrt(ms + EPS) * sg_ref[...]) * (1.0 - LAM_INIT)
        o_ref[0, j * tqs:(j + 1) * tqs, :] = yt.T.astype(BF16)


def _attention(q, kx, km, vxt, vmt, lams, sg, nb, tq):
    qb, _, nq, _ = q.shape
    s = kx.shape[2]
    grid = (nb, HEADS, nq // tq)
    qmap = (lambda b, h, i: (b, h, i, 0)) if qb == nb else (lambda b, h, i: (0, h, i, 0))
    vec = lambda w: pl.BlockSpec((1, w), lambda b, h, i: (0, 0))
    return pl.pallas_call(
        _attn_kernel,
        grid=grid,
        in_specs=[
            pl.BlockSpec((1, 1, tq, LANES), qmap),
            pl.BlockSpec((1, 1, s, LANES), lambda b, h, i: (b, h, 0, 0)),
            pl.BlockSpec((1, 1, LANES, LANES), lambda b, h, i: (0, h, 0, 0)),
            pl.BlockSpec((1, 1, VALUE_DIM, s), lambda b, h, i: (b, h, 0, 0)),
            pl.BlockSpec((1, 1, VALUE_DIM, LANES), lambda b, h, i: (0, h, 0, 0)),
            vec(HEAD_DIM), vec(HEAD_DIM), vec(HEAD_DIM), vec(HEAD_DIM),
            pl.BlockSpec((VALUE_DIM, 1), lambda b, h, i: (0, 0)),
        ],
        out_specs=pl.BlockSpec((1, tq, VALUE_DIM), lambda b, h, i: (b, i, h)),
        out_shape=jax.ShapeDtypeStruct((nb, nq, ATTN_WIDTH), BF16),
        compiler_params=_cparams(3),
        name="diff_attn",
    )(q, kx, km, vxt, vmt, *lams, sg)


def _pool_kernel(pm_ref, px_ref, pw_ref, ps_ref, ox_ref, om_ref, seq_ref, *, seq_len):
    lp = seq_ref.shape[0]
    s = px_ref.shape[1]
    seq_ref[0:POOL_PAD, :] = jnp.zeros((POOL_PAD, POOL_WIDTH), F32)
    seq_ref[POOL_PAD + seq_len:lp, :] = jnp.zeros((lp - POOL_PAD - seq_len, POOL_WIDTH), F32)
    seq_ref[POOL_PAD:POOL_PAD + N_META, :] = pm_ref[...]
    seq_ref[POOL_PAD + N_META:POOL_PAD + seq_len, :] = px_ref[0]
    t = lax.broadcasted_iota(jnp.int32, (lp, 1), 0) - POOL_PAD
    for g, w in enumerate(POOL_WINDOWS):
        c0, c1 = g * POOL_GROUP_DIM, (g + 1) * POOL_GROUP_DIM
        sg = seq_ref[:, c0:c1]
        win = sg + pltpu.roll(sg, 1, 0)
        span = 1
        while 2 * span < w:
            win = pltpu.roll(win, span, 0) + pltpu.roll(win, lp - span, 0)
            span *= 2
        cnt = jnp.minimum(t + w // 2, seq_len) - jnp.maximum(t - w // 2, 0)
        cnt = jnp.maximum(cnt, 1).astype(F32)
        d = (win / cnt - sg).astype(BF16)
        y = _dot(d, pw_ref[g]) * ps_ref[:, c0:c1]
        om_ref[0, :, c0:c1] = y[POOL_PAD:POOL_PAD + N_META].astype(BF16)
        ox_ref[0, :, c0:c1] = y[POOL_PAD + N_META:POOL_PAD + N_META + s].astype(BF16)


def _pool(p_m, p_x, pool_w, pool_scale):
    nb, s, _ = p_x.shape
    seq_len = s + N_META
    lp = seq_len + 2 * POOL_PAD
    return pl.pallas_call(
        functools.partial(_pool_kernel, seq_len=seq_len),
        grid=(nb,),
        in_specs=[
            pl.BlockSpec((N_META, POOL_WIDTH), lambda b: (0, 0)),
            pl.BlockSpec((1, s, POOL_WIDTH), lambda b: (b, 0, 0)),
            pl.BlockSpec((len(POOL_WINDOWS), POOL_GROUP_DIM, POOL_GROUP_DIM), lambda b: (0, 0, 0)),
            pl.BlockSpec((1, POOL_WIDTH), lambda b: (0, 0)),
        ],
        out_specs=[pl.BlockSpec((1, s, POOL_WIDTH), lambda b: (b, 0, 0)),
                   pl.BlockSpec((1, N_META, POOL_WIDTH), lambda b: (b, 0, 0))],
        out_shape=[jax.ShapeDtypeStruct((nb, s, POOL_WIDTH), BF16),
                   jax.ShapeDtypeStruct((nb, N_META, POOL_WIDTH), BF16)],
        scratch_shapes=[pltpu.VMEM((lp, POOL_WIDTH), F32)],
        compiler_params=_cparams(1),
        name="pool_mixer",
    )(p_m, p_x, pool_w, pool_scale)


def _outproj_kernel(a_ref, p_ref, x_ref, wo_ref, g2_ref, wrh_ref, wrl_ref,
                    h_ref, hn_ref, aff_ref):
    mix = _dot(a_ref[0], wo_ref[0:ATTN_WIDTH, :]) + _dot(p_ref[0], wo_ref[ATTN_WIDTH:, :])
    h = x_ref[0] + mix
    h_ref[0] = h
    ms = jnp.mean(h * h, axis=-1, keepdims=True)
    hn = h * lax.rsqrt(ms + EPS) * g2_ref[...]
    hn_ref[0] = hn
    hi = hn.astype(BF16)
    lo = (hn - hi.astype(F32)).astype(BF16)
    wrh = wrh_ref[...]
    logits = _dot_nt(wrh, hi) + _dot_nt(wrl_ref[...], hi) + _dot_nt(wrh, lo)
    m = jnp.max(logits, axis=0, keepdims=True)
    e = jnp.exp(logits - m)
    aff_ref[0] = e / jnp.sum(e, axis=0, keepdims=True)


def _out_proj(attn, pool, x, w_out, g2, wr_hi, wr_lo, tm):
    nb, n, _ = x.shape
    grid = (nb, n // tm)
    const = lambda shape: pl.BlockSpec(shape, lambda b, i: (0,) * len(shape))
    return pl.pallas_call(
        _outproj_kernel,
        grid=grid,
        in_specs=[
            pl.BlockSpec((1, tm, ATTN_WIDTH), lambda b, i: (b, i, 0)),
            pl.BlockSpec((1, tm, POOL_WIDTH), lambda b, i: (b, i, 0)),
            pl.BlockSpec((1, tm, D_MODEL), lambda b, i: (b, i, 0)),
            const((D_MODEL, D_MODEL)),
            const((1, D_MODEL)),
            const((N_EXPERTS, D_MODEL)),
            const((N_EXPERTS, D_MODEL)),
        ],
        out_specs=[pl.BlockSpec((1, tm, D_MODEL), lambda b, i: (b, i, 0)),
                   pl.BlockSpec((1, tm, D_MODEL), lambda b, i: (b, i, 0)),
                   pl.BlockSpec((1, N_EXPERTS, tm), lambda b, i: (b, 0, i))],
        out_shape=[jax.ShapeDtypeStruct((nb, n, D_MODEL), F32),
                   jax.ShapeDtypeStruct((nb, n, D_MODEL), F32),
                   jax.ShapeDtypeStruct((nb, N_EXPERTS, n), F32)],
        compiler_params=_cparams(2),
        name="out_proj_router",
    )(attn, pool, x, w_out, g2, wr_hi, wr_lo)


def _select_kernel(aff_ref, u_ref, slot_ref, win_ref, *, cap, n_tok):
    aff = aff_ref[...]
    rows, lpad = aff.shape

    def body(i, thr):
        cand = thr | jnp.left_shift(jnp.int32(1), 30 - i)
        cand_f = lax.bitcast_convert_type(cand, F32)
        cnt = jnp.sum((aff >= cand_f).astype(jnp.int32), axis=1, keepdims=True)
        return jnp.where(cnt >= cap, cand, thr)

    thr = lax.fori_loop(0, 31, body, jnp.zeros((rows, 1), jnp.int32))
    thr_f = lax.bitcast_convert_type(thr, F32)
    gt = aff > thr_f
    eq = aff == thr_f
    n_gt = jnp.sum(gt.astype(jnp.int32), axis=1, keepdims=True)
    u = u_ref[...]
    rank_eq = _dot(eq.astype(BF16), u)
    sel = gt | (eq & (rank_eq < (cap - n_gt).astype(F32)))
    pos = _dot(sel.astype(BF16), u)
    slot = jnp.where(sel, pos.astype(jnp.int32), -1)
    slot_ref[...] = slot

    lane = lax.broadcasted_iota(jnp.int32, (rows, lpad), 1)
    out_lane = lax.broadcasted_iota(jnp.int32, (rows, LANES), 1)
    win = jnp.zeros((rows, LANES), jnp.int32)
    n_tiles = n_tok // COMBINE_TILE
    big = jnp.int32(1 << 20)
    for j in range(n_tiles):
        t0 = N_META + j * COMBINE_TILE
        inside = (lane >= t0) & (lane < t0 + COMBINE_TILE) & sel
        lo = jnp.min(jnp.where(inside, slot, big), axis=1, keepdims=True)
        hi = jnp.max(jnp.where(inside, slot, -1), axis=1, keepdims=True)
        first = jnp.where(lo == big, 0, (lo // SLOT_ALIGN) * SLOT_ALIGN)
        win = jnp.where(out_lane == j, first, win)
        win = jnp.where(out_lane == WIN_HI_LANE + j, hi, win)
    win_ref[...] = win


def _select(aff_rows, u, cap, rb, n_tok):
    r, lpad = aff_rows.shape
    assert n_tok // COMBINE_TILE <= WIN_HI_LANE
    return pl.pallas_call(
        functools.partial(_select_kernel, cap=cap, n_tok=n_tok),
        grid=(r // rb,),
        in_specs=[pl.BlockSpec((rb, lpad), lambda i: (i, 0)),
                  pl.BlockSpec((lpad, lpad), lambda i: (0, 0))],
        out_specs=[pl.BlockSpec((rb, lpad), lambda i: (i, 0)),
                   pl.BlockSpec((rb, LANES), lambda i: (i, 0))],
        out_shape=[jax.ShapeDtypeStruct((r, lpad), jnp.int32),
                   jax.ShapeDtypeStruct((r, LANES), jnp.int32)],
        compiler_params=_cparams(1),
        name="expert_choice_select",
    )(aff_rows, u)


def _slot_values(slot_row, value_row):
    s = slot_row.shape[1]
    c = lax.broadcasted_iota(jnp.int32, (SLOT_ROWS, LANES), 0)
    acc = jnp.zeros((SLOT_ROWS, LANES), F32)
    for t0 in range(0, s, LANES):
        hit = slot_row[:, t0:t0 + LANES] == c
        acc = jnp.where(hit, value_row[:, t0:t0 + LANES], acc)
    return jnp.sum(acc, axis=1, keepdims=True)


def _slot_tokens_kernel(slot_ref, idx_ref):
    s = slot_ref.shape[2]
    tok = lax.broadcasted_iota(jnp.int32, (1, s), 1).astype(F32)
    for e in range(N_EXPERTS):
        idx_col = _slot_values(slot_ref[e], tok)
        head = jnp.broadcast_to(idx_col[0:MXU_DIM], (MXU_DIM, LANES)).T[0:1]
        tail = jnp.broadcast_to(idx_col[SLOT_ROWS - LANES:SLOT_ROWS], (LANES, LANES)).T[0:1]
        idx_ref[e, :, 0:MXU_DIM] = head.astype(jnp.int32)
        idx_ref[e, :, MXU_DIM:SLOT_LANES] = pltpu.roll(tail, SLOT_ROWS - MXU_DIM, 1).astype(jnp.int32)


def _slot_tokens(slot3):
    r, _, s = slot3.shape
    nb = r // N_EXPERTS
    rows = pl.BlockSpec((N_EXPERTS, 1, s), lambda b: (b, 0, 0))
    return pl.pallas_call(
        _slot_tokens_kernel,
        grid=(nb,),
        in_specs=[rows],
        out_specs=pl.BlockSpec((N_EXPERTS, 1, SLOT_LANES), lambda b: (b, 0, 0)),
        out_shape=jax.ShapeDtypeStruct((r, 1, SLOT_LANES), jnp.int32),
        compiler_params=_cparams(1),
        name="slot_tokens",
    )(slot3)


def _expert_kernel(idx_ref, slot_ref, aff_ref, hn_ref, wg32_ref, wu32_ref, wd32_ref, y_ref,
                   wg_s, wu_s, wd_s, rows_s):
    p = pl.program_id(0)
    b = pl.program_id(1)
    nb = pl.num_programs(1)
    fill = p % 2
    cur = (p * nb + b) % 2

    def gather_next():
        for c in range(SLOT_ROWS):
            first = pl.multiple_of(idx_ref[0, 0, c] * WORD_ROWS, WORD_ROWS)
            rows_s[1 - cur, pl.ds(c, WORD_ROWS, stride=GATHER_STRIDE), :] = hn_ref[0, pl.ds(first, WORD_ROWS), :]

    @pl.when(p < N_EXPERTS)
    def _():
        rg = wg32_ref.shape[1]
        rd = wd32_ref.shape[1]
        wg_s[fill, pl.ds(pl.multiple_of(b * rg, rg), rg), :] = wg32_ref[0].astype(BF16)
        wu_s[fill, pl.ds(pl.multiple_of(b * rg, rg), rg), :] = wu32_ref[0].astype(BF16)
        wd_s[fill, pl.ds(pl.multiple_of(b * rd, rd), rd), :] = wd32_ref[0].astype(BF16)

    @pl.when(p == 0)
    def _():
        y_ref[0] = jnp.zeros(y_ref.shape[1:], BF16)

    @pl.when((p == 0) & (b == nb - 1))
    def _():
        gather_next()

    @pl.when(p > 0)
    def _():
        use = 1 - fill
        xg = jnp.concatenate([rows_s[cur, pl.ds(j * GATHER_STRIDE, SLOT_ROWS), :] for j in range(WORD_ROWS)],
                             axis=1).astype(BF16)
        gather_next()
        g = _dot(xg, wg_s[use])
        u = _dot(xg, wu_s[use])
        a = (g * jax.nn.sigmoid(g) * u).astype(BF16)
        gate = _slot_values(slot_ref[0], aff_ref[0])
        y_ref[0] = (_dot(a, wd_s[use]) * gate).astype(BF16)


def _experts(idx3, slot3, aff3, hn_words, wg, wu, wd):
    nb = hn_words.shape[0]
    s = slot3.shape[2]
    assert D_MODEL % nb == 0 and (D_MODEL // nb) % 16 == 0
    rg = D_MODEL // nb
    rd = D_FF // nb
    grid = (N_EXPERTS + 1, nb)
    last = N_EXPERTS - 1

    def next_step(p, b):
        t = p * nb + b + 1
        return t % nb, jnp.clip(t // nb - 1, 0, last)

    def next_rows(p, b):
        b1, e1 = next_step(p, b)
        return b1 * N_EXPERTS + e1, 0, 0

    wmap = lambda p, b: (jnp.minimum(p, last), jnp.where(p <= last, b, nb - 1), 0)
    row = pl.BlockSpec((1, 1, s), lambda p, b: (b * N_EXPERTS + jnp.maximum(p - 1, 0), 0, 0))
    return pl.pallas_call(
        _expert_kernel,
        grid=grid,
        in_specs=[
            pl.BlockSpec((1, 1, SLOT_LANES), next_rows, memory_space=pltpu.SMEM),
            row, row,
            pl.BlockSpec((1,) + hn_words.shape[1:], lambda p, b: (next_step(p, b)[0], 0, 0)),
            pl.BlockSpec((1, rg, D_FF), wmap),
            pl.BlockSpec((1, rg, D_FF), wmap),
            pl.BlockSpec((1, rd, D_MODEL), wmap),
        ],
        out_specs=pl.BlockSpec((1, SLOT_ROWS, D_MODEL), lambda p, b: (b, p, 0)),
        out_shape=jax.ShapeDtypeStruct((nb, (N_EXPERTS + 1) * SLOT_ROWS, D_MODEL), BF16),
        scratch_shapes=[pltpu.VMEM((2, D_MODEL, D_FF), BF16),
                        pltpu.VMEM((2, D_MODEL, D_FF), BF16),
                        pltpu.VMEM((2, D_FF, D_MODEL), BF16),
                        pltpu.VMEM((2, GATHER_ROWS, LANES), F32)],
        compiler_params=_cparams(2),
        name="routed_experts",
    )(idx3, slot3, aff3, hn_words, wg, wu, wd)


def _combine_kernel(win_ref, h_ref, st_ref, y_ref, o_ref):
    j = pl.program_id(1)
    st = st_ref[0]
    tm = st.shape[0]
    lane = lax.broadcasted_iota(jnp.int32, (tm, COMBINE_WINDOW), 1)
    firsts = [win_ref[0, e, j] for e in range(N_EXPERTS)]
    n_pass = jnp.int32(1)
    for e in range(N_EXPERTS):
        span = win_ref[0, e, WIN_HI_LANE + j] - firsts[e]
        n_pass = jnp.maximum(n_pass, span // COMBINE_WINDOW + 1)

    per_dot = MXU_DIM // COMBINE_WINDOW

    def one_pass(k, acc):
        for e0 in range(0, N_EXPERTS, per_dot):
            hits, rows = [], []
            for e in range(e0, e0 + per_dot):
                fresh = firsts[e] + k * COMBINE_WINDOW
                start = pl.multiple_of(jnp.minimum(fresh, SLOT_ROWS - COMBINE_WINDOW), SLOT_ALIGN)
                rows.append(y_ref[0, pl.ds(pl.multiple_of((e + 1) * SLOT_ROWS + start, SLOT_ALIGN),
                                           COMBINE_WINDOW), :])
                col = st[:, e:e + 1]
                hits.append(((col == start + lane) & (col >= fresh)).astype(BF16))
            acc = acc + _dot(jnp.concatenate(hits, axis=1), jnp.concatenate(rows, axis=0))
        return acc

    o_ref[0] = lax.fori_loop(0, n_pass, one_pass, h_ref[0])


def _combine(h, slot_t, win, y):
    nb, s, _ = h.shape
    tm = COMBINE_TILE
    return pl.pallas_call(
        _combine_kernel,
        grid=(nb, s // tm),
        in_specs=[
            pl.BlockSpec((1, N_EXPERTS, LANES), lambda b, i: (b, 0, 0), memory_space=pltpu.SMEM),
            pl.BlockSpec((1, tm, D_MODEL), lambda b, i: (b, i, 0)),
            pl.BlockSpec((1, tm, N_EXPERTS), lambda b, i: (b, i, 0)),
            pl.BlockSpec((1, (N_EXPERTS + 1) * SLOT_ROWS, D_MODEL), lambda b, i: (b, 0, 0)),
        ],
        out_specs=pl.BlockSpec((1, tm, D_MODEL), lambda b, i: (b, i, 0)),
        out_shape=jax.ShapeDtypeStruct((nb, s, D_MODEL), F32),
        compiler_params=_cparams(2),
        name="moe_combine",
    )(win, h, slot_t, y)


def _rope_tables(pos):
    half = HEAD_DIM // 2
    freqs = ROPE_THETA ** (-jnp.arange(half, dtype=F32) / half)
    ang = pos[:, None] * freqs[None, :]
    cos = jnp.cos(ang)
    sin = jnp.sin(ang)
    cos_t = jnp.concatenate([cos, cos, cos, cos], axis=1)
    sin_t = jnp.concatenate([-sin, sin, -sin, sin], axis=1)
    return cos_t, sin_t


def kernel(x, meta_tokens, ln1_g, w_in, q_norm_g, k_norm_g, lambda_q1, lambda_k1, lambda_q2, lambda_k2,
           subln_g, pool_w, pool_scale, w_out, ln2_g, w_router, w_gate, w_up, w_down):
    nb, s, _ = x.shape
    seq_len = s + N_META
    cap = CAPACITY_FACTOR * seq_len // N_EXPERTS
    assert cap <= SLOT_ROWS and s % COMBINE_TILE == 0
    l = 0

    w_in_b = w_in[l].astype(BF16)
    w_out_b = w_out[l].astype(BF16)
    pw_b = pool_w[l].astype(BF16)
    wr_t = w_router[l].T
    wr_hi = wr_t.astype(BF16)
    wr_lo = (wr_t - wr_hi.astype(F32)).astype(BF16)
    g1 = ln1_g[l][None]
    g2 = ln2_g[l][None]
    qg = jnp.tile(q_norm_g[l], 2)[None]
    kg = jnp.tile(k_norm_g[l], 2)[None]
    sg = subln_g[l][:, None]
    ps = pool_scale[l][None]
    lams = (lambda_q1[l][None], lambda_k1[l][None], lambda_q2[l][None], lambda_k2[l][None])
    lane = jnp.arange(MXU_DIM)
    gm = (lane[:, None] // HEAD_DIM == lane[None, :] // HEAD_DIM).astype(BF16)
    cos_x, sin_x = _rope_tables(jnp.arange(N_META, seq_len, dtype=F32))
    cos_m, sin_m = _rope_tables(jnp.arange(N_META, dtype=F32))

    tm = _row_tile(s, 512)
    wvt_b = w_in_b[:, 2 * ATTN_WIDTH:3 * ATTN_WIDTH].T
    q_x, k_x, vt_x, p_x = _in_proj(x, g1, w_in_b, wvt_b, cos_x, sin_x, qg, kg, gm, tm)
    q_m, k_m, vt_m, p_m = _in_proj(meta_tokens[None].astype(x.dtype), g1, w_in_b, wvt_b, cos_m, sin_m,
                                   qg, kg, gm, N_META)
    pad_rows = ((0, 0), (0, 0), (0, LANES - N_META), (0, 0))
    q_mp = jnp.pad(q_m, pad_rows)
    k_mp = jnp.pad(k_m, pad_rows)
    vt_mp = jnp.pad(vt_m, ((0, 0), (0, 0), (0, 0), (0, LANES - N_META)))

    attn_x = _attention(q_x, k_x, k_mp, vt_x, vt_mp, lams, sg, nb, _row_tile(s, 2048))
    attn_m = _attention(q_mp, k_x, k_mp, vt_x, vt_mp, lams, sg, nb, LANES)[:, :N_META]

    pool_x, pool_m = _pool(p_m[0], p_x, pw_b, ps)

    h_x, hn_x, aff_x = _out_proj(attn_x, pool_x, x, w_out_b, g2, wr_hi, wr_lo, tm)
    n_mrows = nb * N_META
    meta_rows = jnp.broadcast_to(meta_tokens[None].astype(x.dtype), (nb, N_META, D_MODEL)).reshape(1, n_mrows, D_MODEL)
    _, _, aff_m = _out_proj(attn_m.reshape(1, n_mrows, ATTN_WIDTH), pool_m.reshape(1, n_mrows, POOL_WIDTH),
                            meta_rows, w_out_b, g2, wr_hi, wr_lo, n_mrows)
    aff_m = aff_m.reshape(N_EXPERTS, nb, N_META).transpose(1, 0, 2)

    lpad = -(-seq_len // LANES) * LANES
    aff_seq = jnp.concatenate([aff_m, aff_x], axis=2)
    aff_rows = jnp.pad(aff_seq, ((0, 0), (0, 0), (0, lpad - seq_len)), constant_values=-1.0)
    aff_rows = aff_rows.reshape(nb * N_EXPERTS, lpad)
    idx = jnp.arange(lpad)
    upper = (idx[:, None] < idx[None, :]).astype(BF16)
    slot, win = _select(aff_rows, upper, cap, min(nb * N_EXPERTS, 128), s)
    slot_x = slot[:, N_META:seq_len]

    slot3 = slot_x.reshape(nb * N_EXPERTS, 1, s)
    y = _experts(_slot_tokens(slot3), slot3, aff_x.reshape(nb * N_EXPERTS, 1, s),
                 hn_x.reshape(nb, s * WORD_ROWS, LANES), w_gate[l], w_up[l], w_down[l])

    slot_t = slot_x.reshape(nb, N_EXPERTS, s).transpose(0, 2, 1)
    return _combine(h_x, slot_t, win.reshape(nb, N_EXPERTS, LANES), y)
```

```python
import functools
import math

import jax
import jax.numpy as jnp
from jax import lax
from jax.experimental import pallas as pl
from jax.experimental.pallas import tpu as pltpu

F32 = jnp.float32
BF16 = jnp.bfloat16

D_MODEL = 1024
N_META = 16
ATTN_WIDTH = 512
POOL_WIDTH = 512
HEAD_DIM = 64
HEADS = 4
VALUE_DIM = 128
POOL_WINDOWS = (2, 4, 8, 16)
POOL_GROUP_DIM = 128
IN_WIDTH = 3 * ATTN_WIDTH + POOL_WIDTH
N_EXPERTS = 16
CAPACITY_FACTOR = 2
D_FF = 2048
ROPE_THETA = 10000.0
EPS = 1e-6
LAM_INIT = 0.8 - 0.6 * math.exp(-0.3 * 0)

LANES = 128
MXU_DIM = 256
LANE_TILES = D_MODEL // LANES
SLOT_ALIGN = 16
SLOT_ROWS = 272
GATHER_STRIDE = SLOT_ROWS + 1
GATHER_ROWS = -(-LANE_TILES * GATHER_STRIDE // 8) * 8
SLOT_LANES = 384
COMBINE_TILE = 256
COMBINE_WINDOW = 128
WIN_HI_LANE = 64
POOL_PAD = 32
SCORE_LOOKAHEAD = 3
VMEM_LIMIT = 56 * 1024 * 1024


def _row_tile(n, target):
    t = min(n, target)
    assert n % t == 0
    return t


def _cparams(n_axes):
    return pltpu.CompilerParams(dimension_semantics=("arbitrary",) * n_axes,
                                vmem_limit_bytes=VMEM_LIMIT)


def _dot(a, b):
    return jnp.dot(a, b, preferred_element_type=F32)


def _dot_nt(a, b):
    return lax.dot_general(a, b, (((1,), (1,)), ((), ())), preferred_element_type=F32)


def _inproj_kernel(x_ref, g1_ref, w_ref, wvt_ref, cos_ref, sin_ref, qg_ref, kg_ref, gm_ref,
                   q_ref, k_ref, vt_ref, p_ref):
    x = x_ref[0]
    ms = jnp.mean(x * x, axis=-1, keepdims=True)
    xn = (x * lax.rsqrt(ms + EPS) * g1_ref[...]).astype(BF16)
    cos = cos_ref[...]
    sin = sin_ref[...]
    gm = gm_ref[...]
    tm = x.shape[0]
    lane = lax.broadcasted_iota(jnp.int32, (tm, LANES), 1)
    low_half = (lane & 32) == 0

    def norm_rope(t, ss, g, scale):
        tn = t * lax.rsqrt(ss * (1.0 / HEAD_DIM) + EPS) * g
        sw = jnp.where(low_half, pltpu.roll(tn, LANES - 32, 1), pltpu.roll(tn, 32, 1))
        r = tn * cos + sw * sin
        if scale != 1.0:
            r = r * scale
        return r.astype(BF16)

    pq = _dot(xn, w_ref[:, 0:ATTN_WIDTH])
    pk = _dot(xn, w_ref[:, ATTN_WIDTH:2 * ATTN_WIDTH])
    for h in range(HEADS):
        tq = pq[:, h * LANES:(h + 1) * LANES]
        tk = pk[:, h * LANES:(h + 1) * LANES]
        sq = jnp.concatenate([tq * tq, tk * tk], axis=1)
        hi = sq.astype(BF16)
        lo = (sq - hi.astype(F32)).astype(BF16)
        ss = _dot(hi, gm) + _dot(lo, gm)
        q_ref[0, h] = norm_rope(tq, ss[:, 0:LANES], qg_ref[...], HEAD_DIM ** -0.5 * math.log2(math.e))
        k_ref[0, h] = norm_rope(tk, ss[:, LANES:2 * LANES], kg_ref[...], 1.0)
    pvt = _dot_nt(wvt_ref[...], xn)
    for h in range(HEADS):
        vt_ref[0, h] = pvt[h * VALUE_DIM:(h + 1) * VALUE_DIM, :].astype(BF16)
    p_ref[0] = _dot(xn, w_ref[:, 3 * ATTN_WIDTH:IN_WIDTH])


def _in_proj(x, g1, w_in, wvt, cos, sin, qg, kg, gm, tm):
    nb, n, _ = x.shape
    grid = (nb, n // tm)
    hd = jax.ShapeDtypeStruct((nb, HEADS, n, LANES), BF16)
    hdt = jax.ShapeDtypeStruct((nb, HEADS, VALUE_DIM, n), BF16)
    head_spec = pl.BlockSpec((1, HEADS, tm, LANES), lambda b, i: (b, 0, i, 0))
    headt_spec = pl.BlockSpec((1, HEADS, VALUE_DIM, tm), lambda b, i: (b, 0, 0, i))
    const = lambda shape: pl.BlockSpec(shape, lambda b, i: (0,) * len(shape))
    return pl.pallas_call(
        _inproj_kernel,
        grid=grid,
        in_specs=[
            pl.BlockSpec((1, tm, D_MODEL), lambda b, i: (b, i, 0)),
            const((1, D_MODEL)),
            const((D_MODEL, IN_WIDTH)),
            const((ATTN_WIDTH, D_MODEL)),
            pl.BlockSpec((tm, LANES), lambda b, i: (i, 0)),
            pl.BlockSpec((tm, LANES), lambda b, i: (i, 0)),
            const((1, LANES)),
            const((1, LANES)),
            const((MXU_DIM, MXU_DIM)),
        ],
        out_specs=[head_spec, head_spec, headt_spec,
                   pl.BlockSpec((1, tm, POOL_WIDTH), lambda b, i: (b, i, 0))],
        out_shape=[hd, hd, hdt, jax.ShapeDtypeStruct((nb, n, POOL_WIDTH), F32)],
        compiler_params=_cparams(2),
        name="in_proj",
    )(x, g1, w_in, wvt, cos, sin, qg, kg, gm)


def _attn_kernel(q_ref, kx_ref, km_ref, vxt_ref, vmt_ref, lq1_ref, lk1_ref, lq2_ref, lk2_ref,
                 sg_ref, o_ref):
    q = q_ref[0, 0]
    tq = q.shape[0]
    lane = lax.broadcasted_iota(jnp.int32, (tq, LANES), 1)
    zero = jnp.zeros_like(q)
    q_maps = (jnp.where(lane < HEAD_DIM, q, zero), jnp.where(lane >= HEAD_DIM, q, zero))
    kx = kx_ref[0, 0]
    km = km_ref[0, 0]
    vxt = vxt_ref[0, 0]
    vmt = vmt_ref[0, 0]
    tqs = MXU_DIM if tq % MXU_DIM == 0 else tq
    n_sub = tq // tqs
    meta_valid = lax.broadcasted_iota(jnp.int32, (LANES, tqs), 0) < N_META

    def scores(qq):
        sx = _dot_nt(kx, qq)
        sm = jnp.where(meta_valid, _dot_nt(km, qq), -1e30)
        m = jnp.maximum(jnp.max(sx, axis=0, keepdims=True), jnp.max(sm, axis=0, keepdims=True))
        return sx, sm, m

    def weighted_values(sx, sm, m):
        px = jnp.exp2(sx - m)
        pm = jnp.exp2(sm - m)
        l = jnp.sum(px, axis=0, keepdims=True) + jnp.sum(pm, axis=0, keepdims=True)
        ot = _dot(vxt, px.astype(BF16)) + _dot(vmt, pm.astype(BF16))
        return ot / l

    units = [(j, mp) for j in range(n_sub) for mp in range(2)]
    outs = {}
    pending = []
    for j, mp in units:
        pending.append(((j, mp), scores(q_maps[mp][j * tqs:(j + 1) * tqs])))
        if len(pending) > SCORE_LOOKAHEAD:
            key, args = pending.pop(0)
            outs[key] = weighted_values(*args)
    for key, args in pending:
        outs[key] = weighted_values(*args)

    lam = (jnp.exp(jnp.sum(lq1_ref[...] * lk1_ref[...], axis=-1, keepdims=True))
           - jnp.exp(jnp.sum(lq2_ref[...] * lk2_ref[...], axis=-1, keepdims=True))
           + LAM_INIT)
    for j in range(n_sub):
        ot = outs[(j, 0)] - lam * outs[(j, 1)]
        ms = jnp.mean(ot * ot, axis=0, keepdims=True)
        yt = (ot * lax.rsqrt(ms + EPS) * sg_ref[...]) * (1.0 - LAM_INIT)
        o_ref[0, j * tqs:(j + 1) * tqs, :] = yt.T.astype(BF16)


def _attention(q, kx, km, vxt, vmt, lams, sg, nb, tq):
    qb, _, nq, _ = q.shape
    s = kx.shape[2]
    grid = (nb, HEADS, nq // tq)
    qmap = (lambda b, h, i: (b, h, i, 0)) if qb == nb else (lambda b, h, i: (0, h, i, 0))
    vec = lambda w: pl.BlockSpec((1, w), lambda b, h, i: (0, 0))
    return pl.pallas_call(
        _attn_kernel,
        grid=grid,
        in_specs=[
            pl.BlockSpec((1, 1, tq, LANES), qmap),
            pl.BlockSpec((1, 1, s, LANES), lambda b, h, i: (b, h, 0, 0)),
            pl.BlockSpec((1, 1, LANES, LANES), lambda b, h, i: (0, h, 0, 0)),
            pl.BlockSpec((1, 1, VALUE_DIM, s), lambda b, h, i: (b, h, 0, 0)),
            pl.BlockSpec((1, 1, VALUE_DIM, LANES), lambda b, h, i: (0, h, 0, 0)),
            vec(HEAD_DIM), vec(HEAD_DIM), vec(HEAD_DIM), vec(HEAD_DIM),
            pl.BlockSpec((VALUE_DIM, 1), lambda b, h, i: (0, 0)),
        ],
        out_specs=pl.BlockSpec((1, tq, VALUE_DIM), lambda b, h, i: (b, i, h)),
        out_shape=jax.ShapeDtypeStruct((nb, nq, ATTN_WIDTH), BF16),
        compiler_params=_cparams(3),
        name="diff_attn",
    )(q, kx, km, vxt, vmt, *lams, sg)


def _pool_kernel(pm_ref, px_ref, pw_ref, ps_ref, ox_ref, om_ref, seq_ref, *, seq_len):
    lp = seq_ref.shape[0]
    s = px_ref.shape[1]
    seq_ref[0:POOL_PAD, :] = jnp.zeros((POOL_PAD, POOL_WIDTH), F32)
    seq_ref[POOL_PAD + seq_len:lp, :] = jnp.zeros((lp - POOL_PAD - seq_len, POOL_WIDTH), F32)
    seq_ref[POOL_PAD:POOL_PAD + N_META, :] = pm_ref[...]
    seq_ref[POOL_PAD + N_META:POOL_PAD + seq_len, :] = px_ref[0]
    t = lax.broadcasted_iota(jnp.int32, (lp, 1), 0) - POOL_PAD
    for g, w in enumerate(POOL_WINDOWS):
        c0, c1 = g * POOL_GROUP_DIM, (g + 1) * POOL_GROUP_DIM
        sg = seq_ref[:, c0:c1]
        win = sg + pltpu.roll(sg, 1, 0)
        span = 1
        while 2 * span < w:
            win = pltpu.roll(win, span, 0) + pltpu.roll(win, lp - span, 0)
            span *= 2
        cnt = jnp.minimum(t + w // 2, seq_len) - jnp.maximum(t - w // 2, 0)
        cnt = jnp.maximum(cnt, 1).astype(F32)
        d = (win / cnt - sg).astype(BF16)
        y = _dot(d, pw_ref[g]) * ps_ref[:, c0:c1]
        om_ref[0, :, c0:c1] = y[POOL_PAD:POOL_PAD + N_META].astype(BF16)
        ox_ref[0, :, c0:c1] = y[POOL_PAD + N_META:POOL_PAD + N_META + s].astype(BF16)


def _pool(p_m, p_x, pool_w, pool_scale):
    nb, s, _ = p_x.shape
    seq_len = s + N_META
    lp = seq_len + 2 * POOL_PAD
    return pl.pallas_call(
        functools.partial(_pool_kernel, seq_len=seq_len),
        grid=(nb,),
        in_specs=[
            pl.BlockSpec((N_META, POOL_WIDTH), lambda b: (0, 0)),
            pl.BlockSpec((1, s, POOL_WIDTH), lambda b: (b, 0, 0)),
            pl.BlockSpec((len(POOL_WINDOWS), POOL_GROUP_DIM, POOL_GROUP_DIM), lambda b: (0, 0, 0)),
            pl.BlockSpec((1, POOL_WIDTH), lambda b: (0, 0)),
        ],
        out_specs=[pl.BlockSpec((1, s, POOL_WIDTH), lambda b: (b, 0, 0)),
                   pl.BlockSpec((1, N_META, POOL_WIDTH), lambda b: (b, 0, 0))],
        out_shape=[jax.ShapeDtypeStruct((nb, s, POOL_WIDTH), BF16),
                   jax.ShapeDtypeStruct((nb, N_META, POOL_WIDTH), BF16)],
        scratch_shapes=[pltpu.VMEM((lp, POOL_WIDTH), F32)],
        compiler_params=_cparams(1),
        name="pool_mixer",
    )(p_m, p_x, pool_w, pool_scale)


def _outproj_kernel(a_ref, p_ref, x_ref, wo_ref, g2_ref, wrh_ref, wrl_ref,
                    h_ref, hn_ref, aff_ref):
    mix = _dot(a_ref[0], wo_ref[0:ATTN_WIDTH, :]) + _dot(p_ref[0], wo_ref[ATTN_WIDTH:, :])
    h = x_ref[0] + mix
    h_ref[0] = h
    ms = jnp.mean(h * h, axis=-1, keepdims=True)
    hn = h * lax.rsqrt(ms + EPS) * g2_ref[...]
    for g in range(hn.shape[0] // 8):
        for j in range(LANE_TILES):
            r0 = (g * LANE_TILES + j) * 8
            hn_ref[0, r0:r0 + 8, :] = hn[g * 8:(g + 1) * 8, j * LANES:(j + 1) * LANES]
    hi = hn.astype(BF16)
    lo = (hn - hi.astype(F32)).astype(BF16)
    wrh = wrh_ref[...]
    logits = _dot_nt(wrh, hi) + _dot_nt(wrl_ref[...], hi) + _dot_nt(wrh, lo)
    m = jnp.max(logits, axis=0, keepdims=True)
    e = jnp.exp(logits - m)
    aff_ref[0] = e / jnp.sum(e, axis=0, keepdims=True)


def _out_proj(attn, pool, x, w_out, g2, wr_hi, wr_lo, tm):
    nb, n, _ = x.shape
    grid = (nb, n // tm)
    const = lambda shape: pl.BlockSpec(shape, lambda b, i: (0,) * len(shape))
    return pl.pallas_call(
        _outproj_kernel,
        grid=grid,
        in_specs=[
            pl.BlockSpec((1, tm, ATTN_WIDTH), lambda b, i: (b, i, 0)),
            pl.BlockSpec((1, tm, POOL_WIDTH), lambda b, i: (b, i, 0)),
            pl.BlockSpec((1, tm, D_MODEL), lambda b, i: (b, i, 0)),
            const((D_MODEL, D_MODEL)),
            const((1, D_MODEL)),
            const((N_EXPERTS, D_MODEL)),
            const((N_EXPERTS, D_MODEL)),
        ],
        out_specs=[pl.BlockSpec((1, tm, D_MODEL), lambda b, i: (b, i, 0)),
                   pl.BlockSpec((1, tm * LANE_TILES, LANES), lambda b, i: (b, i, 0)),
                   pl.BlockSpec((1, N_EXPERTS, tm), lambda b, i: (b, 0, i))],
        out_shape=[jax.ShapeDtypeStruct((nb, n, D_MODEL), F32),
                   jax.ShapeDtypeStruct((nb, n * LANE_TILES, LANES), F32),
                   jax.ShapeDtypeStruct((nb, N_EXPERTS, n), F32)],
        compiler_params=_cparams(2),
        name="out_proj_router",
    )(attn, pool, x, w_out, g2, wr_hi, wr_lo)


def _select_kernel(aff_ref, u_ref, slot_ref, win_ref, *, cap, n_tok):
    aff = aff_ref[...]
    rows, lpad = aff.shape

    def body(i, thr):
        cand = thr | jnp.left_shift(jnp.int32(1), 30 - i)
        cand_f = lax.bitcast_convert_type(cand, F32)
        cnt = jnp.sum((aff >= cand_f).astype(jnp.int32), axis=1, keepdims=True)
        return jnp.where(cnt >= cap, cand, thr)

    thr = lax.fori_loop(0, 31, body, jnp.zeros((rows, 1), jnp.int32))
    thr_f = lax.bitcast_convert_type(thr, F32)
    gt = aff > thr_f
    eq = aff == thr_f
    n_gt = jnp.sum(gt.astype(jnp.int32), axis=1, keepdims=True)
    u = u_ref[...]
    rank_eq = _dot(eq.astype(BF16), u)
    sel = gt | (eq & (rank_eq < (cap - n_gt).astype(F32)))
    pos = _dot(sel.astype(BF16), u)
    slot = jnp.where(sel, pos.astype(jnp.int32), -1)
    slot_ref[...] = slot

    lane = lax.broadcasted_iota(jnp.int32, (rows, lpad), 1)
    out_lane = lax.broadcasted_iota(jnp.int32, (rows, LANES), 1)
    win = jnp.zeros((rows, LANES), jnp.int32)
    n_tiles = n_tok // COMBINE_TILE
    big = jnp.int32(1 << 20)
    for j in range(n_tiles):
        t0 = N_META + j * COMBINE_TILE
        inside = (lane >= t0) & (lane < t0 + COMBINE_TILE) & sel
        lo = jnp.min(jnp.where(inside, slot, big), axis=1, keepdims=True)
        hi = jnp.max(jnp.where(inside, slot, -1), axis=1, keepdims=True)
        first = jnp.where(lo == big, 0, (lo // SLOT_ALIGN) * SLOT_ALIGN)
        win = jnp.where(out_lane == j, first, win)
        win = jnp.where(out_lane == WIN_HI_LANE + j, hi, win)
    win_ref[...] = win


def _select(aff_rows, u, cap, rb, n_tok):
    r, lpad = aff_rows.shape
    assert n_tok // COMBINE_TILE <= WIN_HI_LANE
    return pl.pallas_call(
        functools.partial(_select_kernel, cap=cap, n_tok=n_tok),
        grid=(r // rb,),
        in_specs=[pl.BlockSpec((rb, lpad), lambda i: (i, 0)),
                  pl.BlockSpec((lpad, lpad), lambda i: (0, 0))],
        out_specs=[pl.BlockSpec((rb, lpad), lambda i: (i, 0)),
                   pl.BlockSpec((rb, LANES), lambda i: (i, 0))],
        out_shape=[jax.ShapeDtypeStruct((r, lpad), jnp.int32),
                   jax.ShapeDtypeStruct((r, LANES), jnp.int32)],
        compiler_params=_cparams(1),
        name="expert_choice_select",
    )(aff_rows, u)


def _slot_values(slot_row, value_row):
    s = slot_row.shape[1]
    c = lax.broadcasted_iota(jnp.int32, (SLOT_ROWS, LANES), 0)
    acc = jnp.zeros((SLOT_ROWS, LANES), F32)
    for t0 in range(0, s, LANES):
        hit = slot_row[:, t0:t0 + LANES] == c
        acc = jnp.where(hit, value_row[:, t0:t0 + LANES], acc)
    return jnp.sum(acc, axis=1, keepdims=True)


def _slot_tokens_kernel(slot_ref, idx_ref):
    s = slot_ref.shape[2]
    tok = lax.broadcasted_iota(jnp.int32, (1, s), 1).astype(F32)
    for e in range(N_EXPERTS):
        idx_col = _slot_values(slot_ref[e], tok)
        head = jnp.broadcast_to(idx_col[0:MXU_DIM], (MXU_DIM, LANES)).T[0:1]
        tail = jnp.broadcast_to(idx_col[SLOT_ROWS - LANES:SLOT_ROWS], (LANES, LANES)).T[0:1]
        idx_ref[e, :, 0:MXU_DIM] = head.astype(jnp.int32)
        idx_ref[e, :, MXU_DIM:SLOT_LANES] = pltpu.roll(tail, SLOT_ROWS - MXU_DIM, 1).astype(jnp.int32)


def _slot_tokens(slot3):
    r, _, s = slot3.shape
    nb = r // N_EXPERTS
    rows = pl.BlockSpec((N_EXPERTS, 1, s), lambda b: (b, 0, 0))
    return pl.pallas_call(
        _slot_tokens_kernel,
        grid=(nb,),
        in_specs=[rows],
        out_specs=pl.BlockSpec((N_EXPERTS, 1, SLOT_LANES), lambda b: (b, 0, 0)),
        out_shape=jax.ShapeDtypeStruct((r, 1, SLOT_LANES), jnp.int32),
        compiler_params=_cparams(1),
        name="slot_tokens",
    )(slot3)


def _expert_kernel(idx_ref, slot_ref, aff_ref, hn_ref, wg32_ref, wu32_ref, wd32_ref, y_ref,
                   wg_s, wu_s, wd_s, rows_s):
    p = pl.program_id(0)
    b = pl.program_id(1)
    nb = pl.num_programs(1)
    fill = p % 2
    cur = (p * nb + b) % 2

    def gather_row(c):
        t = idx_ref[0, 0, c]
        first = lax.shift_right_logical(t, 3) * (8 * LANE_TILES) + (t & 7)
        rows_s[1 - cur, pl.ds(c, LANE_TILES, stride=GATHER_STRIDE), :] = hn_ref[0, pl.ds(first, LANE_TILES, stride=8), :]

    def gather_next():
        for c in range(SLOT_ROWS):
            gather_row(c)

    @pl.when(p < N_EXPERTS)
    def _():
        rg = wg32_ref.shape[1]
        rd = wd32_ref.shape[1]
        wg_s[fill, pl.ds(pl.multiple_of(b * rg, rg), rg), :] = wg32_ref[0].astype(BF16)
        wu_s[fill, pl.ds(pl.multiple_of(b * rg, rg), rg), :] = wu32_ref[0].astype(BF16)
        wd_s[fill, pl.ds(pl.multiple_of(b * rd, rd), rd), :] = wd32_ref[0].astype(BF16)

    @pl.when(p == 0)
    def _():
        y_ref[0] = jnp.zeros(y_ref.shape[1:], BF16)

    @pl.when((p == 0) & (b == nb - 1))
    def _():
        def body(c, carry):
            gather_row(c)
            return carry
        lax.fori_loop(0, SLOT_ROWS, body, 0)

    @pl.when(p > 0)
    def _():
        use = 1 - fill
        xg = jnp.concatenate([rows_s[cur, pl.ds(j * GATHER_STRIDE, SLOT_ROWS), :] for j in range(LANE_TILES)],
                             axis=1).astype(BF16)
        gather_next()
        g = _dot(xg, wg_s[use])
        u = _dot(xg, wu_s[use])
        a = (g * jax.nn.sigmoid(g) * u).astype(BF16)
        gate = _slot_values(slot_ref[0], aff_ref[0])
        y_ref[0] = (_dot(a, wd_s[use]) * gate).astype(BF16)


def _experts(idx3, slot3, aff3, hn_words, wg, wu, wd):
    nb = hn_words.shape[0]
    s = slot3.shape[2]
    assert D_MODEL % nb == 0 and (D_MODEL // nb) % 16 == 0
    rg = D_MODEL // nb
    rd = D_FF // nb
    grid = (N_EXPERTS + 1, nb)
    last = N_EXPERTS - 1

    def next_step(p, b):
        t = p * nb + b + 1
        return t % nb, jnp.clip(t // nb - 1, 0, last)

    def next_rows(p, b):
        b1, e1 = next_step(p, b)
        return b1 * N_EXPERTS + e1, 0, 0

    wmap = lambda p, b: (jnp.minimum(p, last), jnp.where(p <= last, b, nb - 1), 0)
    row = pl.BlockSpec((1, 1, s), lambda p, b: (b * N_EXPERTS + jnp.maximum(p - 1, 0), 0, 0))
    return pl.pallas_call(
        _expert_kernel,
        grid=grid,
        in_specs=[
            pl.BlockSpec((1, 1, SLOT_LANES), next_rows, memory_space=pltpu.SMEM),
            row, row,
            pl.BlockSpec((1,) + hn_words.shape[1:], lambda p, b: (next_step(p, b)[0], 0, 0)),
            pl.BlockSpec((1, rg, D_FF), wmap),
            pl.BlockSpec((1, rg, D_FF), wmap),
            pl.BlockSpec((1, rd, D_MODEL), wmap),
        ],
        out_specs=pl.BlockSpec((1, SLOT_ROWS, D_MODEL), lambda p, b: (b, p, 0)),
        out_shape=jax.ShapeDtypeStruct((nb, (N_EXPERTS + 1) * SLOT_ROWS, D_MODEL), BF16),
        scratch_shapes=[pltpu.VMEM((2, D_MODEL, D_FF), BF16),
                        pltpu.VMEM((2, D_MODEL, D_FF), BF16),
                        pltpu.VMEM((2, D_FF, D_MODEL), BF16),
                        pltpu.VMEM((2, GATHER_ROWS, LANES), F32)],
        compiler_params=_cparams(2),
        name="routed_experts",
    )(idx3, slot3, aff3, hn_words, wg, wu, wd)


def _combine_kernel(win_ref, h_ref, st_ref, y_ref, o_ref):
    j = pl.program_id(1)
    st = st_ref[0]
    tm = st.shape[0]
    lane = lax.broadcasted_iota(jnp.int32, (tm, COMBINE_WINDOW), 1)
    firsts = [win_ref[0, e, j] for e in range(N_EXPERTS)]
    n_pass = jnp.int32(1)
    for e in range(N_EXPERTS):
        span = win_ref[0, e, WIN_HI_LANE + j] - firsts[e]
        n_pass = jnp.maximum(n_pass, span // COMBINE_WINDOW + 1)

    per_dot = MXU_DIM // COMBINE_WINDOW

    def one_pass(k, acc):
        for e0 in range(0, N_EXPERTS, per_dot):
            hits, rows = [], []
            for e in range(e0, e0 + per_dot):
                fresh = firsts[e] + k * COMBINE_WINDOW
                start = pl.multiple_of(jnp.minimum(fresh, SLOT_ROWS - COMBINE_WINDOW), SLOT_ALIGN)
                rows.append(y_ref[0, pl.ds(pl.multiple_of((e + 1) * SLOT_ROWS + start, SLOT_ALIGN),
                                           COMBINE_WINDOW), :])
                col = st[:, e:e + 1]
                hits.append(((col == start + lane) & (col >= fresh)).astype(BF16))
            acc = acc + _dot(jnp.concatenate(hits, axis=1), jnp.concatenate(rows, axis=0))
        return acc

    o_ref[0] = lax.fori_loop(0, n_pass, one_pass, h_ref[0])


def _combine(h, slot_t, win, y):
    nb, s, _ = h.shape
    tm = COMBINE_TILE
    return pl.pallas_call(
        _combine_kernel,
        grid=(nb, s // tm),
        in_specs=[
            pl.BlockSpec((1, N_EXPERTS, LANES), lambda b, i: (b, 0, 0), memory_space=pltpu.SMEM),
            pl.BlockSpec((1, tm, D_MODEL), lambda b, i: (b, i, 0)),
            pl.BlockSpec((1, tm, N_EXPERTS), lambda b, i: (b, i, 0)),
            pl.BlockSpec((1, (N_EXPERTS + 1) * SLOT_ROWS, D_MODEL), lambda b, i: (b, 0, 0)),
        ],
        out_specs=pl.BlockSpec((1, tm, D_MODEL), lambda b, i: (b, i, 0)),
        out_shape=jax.ShapeDtypeStruct((nb, s, D_MODEL), F32),
        compiler_params=_cparams(2),
        name="moe_combine",
    )(win, h, slot_t, y)


def _rope_tables(pos):
    half = HEAD_DIM // 2
    freqs = ROPE_THETA ** (-jnp.arange(half, dtype=F32) / half)
    ang = pos[:, None] * freqs[None, :]
    cos = jnp.cos(ang)
    sin = jnp.sin(ang)
    cos_t = jnp.concatenate([cos, cos, cos, cos], axis=1)
    sin_t = jnp.concatenate([-sin, sin, -sin, sin], axis=1)
    return cos_t, sin_t


def kernel(x, meta_tokens, ln1_g, w_in, q_norm_g, k_norm_g, lambda_q1, lambda_k1, lambda_q2, lambda_k2,
           subln_g, pool_w, pool_scale, w_out, ln2_g, w_router, w_gate, w_up, w_down):
    nb, s, _ = x.shape
    seq_len = s + N_META
    cap = CAPACITY_FACTOR * seq_len // N_EXPERTS
    assert cap <= SLOT_ROWS and s % COMBINE_TILE == 0
    l = 0

    w_in_b = w_in[l].astype(BF16)
    w_out_b = w_out[l].astype(BF16)
    pw_b = pool_w[l].astype(BF16)
    wr_t = w_router[l].T
    wr_hi = wr_t.astype(BF16)
    wr_lo = (wr_t - wr_hi.astype(F32)).astype(BF16)
    g1 = ln1_g[l][None]
    g2 = ln2_g[l][None]
    qg = jnp.tile(q_norm_g[l], 2)[None]
    kg = jnp.tile(k_norm_g[l], 2)[None]
    sg = subln_g[l][:, None]
    ps = pool_scale[l][None]
    lams = (lambda_q1[l][None], lambda_k1[l][None], lambda_q2[l][None], lambda_k2[l][None])
    lane = jnp.arange(MXU_DIM)
    gm = (lane[:, None] // HEAD_DIM == lane[None, :] // HEAD_DIM).astype(BF16)
    cos_x, sin_x = _rope_tables(jnp.arange(N_META, seq_len, dtype=F32))
    cos_m, sin_m = _rope_tables(jnp.arange(N_META, dtype=F32))

    tm = _row_tile(s, 512)
    wvt_b = w_in_b[:, 2 * ATTN_WIDTH:3 * ATTN_WIDTH].T
    q_x, k_x, vt_x, p_x = _in_proj(x, g1, w_in_b, wvt_b, cos_x, sin_x, qg, kg, gm, tm)
    q_m, k_m, vt_m, p_m = _in_proj(meta_tokens[None].astype(x.dtype), g1, w_in_b, wvt_b, cos_m, sin_m,
                                   qg, kg, gm, N_META)
    pad_rows = ((0, 0), (0, 0), (0, LANES - N_META), (0, 0))
    q_mp = jnp.pad(q_m, pad_rows)
    k_mp = jnp.pad(k_m, pad_rows)
    vt_mp = jnp.pad(vt_m, ((0, 0), (0, 0), (0, 0), (0, LANES - N_META)))

    attn_x = _attention(q_x, k_x, k_mp, vt_x, vt_mp, lams, sg, nb, _row_tile(s, 2048))
    attn_m = _attention(q_mp, k_x, k_mp, vt_x, vt_mp, lams, sg, nb, LANES)[:, :N_META]

    pool_x, pool_m = _pool(p_m[0], p_x, pw_b, ps)

    h_x, hn_x, aff_x = _out_proj(attn_x, pool_x, x, w_out_b, g2, wr_hi, wr_lo, tm)
    n_mrows = nb * N_META
    meta_rows = jnp.broadcast_to(meta_tokens[None].astype(x.dtype), (nb, N_META, D_MODEL)).reshape(1, n_mrows, D_MODEL)
    _, _, aff_m = _out_proj(attn_m.reshape(1, n_mrows, ATTN_WIDTH), pool_m.reshape(1, n_mrows, POOL_WIDTH),
                            meta_rows, w_out_b, g2, wr_hi, wr_lo, n_mrows)
    aff_m = aff_m.reshape(N_EXPERTS, nb, N_META).transpose(1, 0, 2)

    lpad = -(-seq_len // LANES) * LANES
    aff_seq = jnp.concatenate([aff_m, aff_x], axis=2)
    aff_rows = jnp.pad(aff_seq, ((0, 0), (0, 0), (0, lpad - seq_len)), constant_values=-1.0)
    aff_rows = aff_rows.reshape(nb * N_EXPERTS, lpad)
    idx = jnp.arange(lpad)
    upper = (idx[:, None] < idx[None, :]).astype(BF16)
    slot, win = _select(aff_rows, upper, cap, min(nb * N_EXPERTS, 128), s)
    slot_x = slot[:, N_META:seq_len]

    slot3 = slot_x.reshape(nb * N_EXPERTS, 1, s)
    y = _experts(_slot_tokens(slot3), slot3, aff_x.reshape(nb * N_EXPERTS, 1, s),
                 hn_x, w_gate[l], w_up[l], w_down[l])

    slot_t = slot_x.reshape(nb, N_EXPERTS, s).transpose(0, 2, 1)
    return _combine(h_x, slot_t, win.reshape(nb, N_EXPERTS, LANES), y)
```

```python
import functools
import math

import jax
import jax.numpy as jnp
from jax import lax
from jax.experimental import pallas as pl
from jax.experimental.pallas import tpu as pltpu

F32 = jnp.float32
BF16 = jnp.bfloat16

D_MODEL = 1024
N_META = 16
ATTN_WIDTH = 512
POOL_WIDTH = 512
HEAD_DIM = 64
HEADS = 4
VALUE_DIM = 128
POOL_WINDOWS = (2, 4, 8, 16)
POOL_GROUP_DIM = 128
IN_WIDTH = 3 * ATTN_WIDTH + POOL_WIDTH
N_EXPERTS = 16
CAPACITY_FACTOR = 2
D_FF = 2048
ROPE_THETA = 10000.0
EPS = 1e-6
LAM_INIT = 0.8 - 0.6 * math.exp(-0.3 * 0)

LANES = 128
MXU_DIM = 256
LANE_TILES = D_MODEL // LANES
SLOT_ALIGN = 16
SLOT_ROWS = 272
GATHER_STRIDE = SLOT_ROWS + 1
GATHER_ROWS = -(-LANE_TILES * GATHER_STRIDE // 8) * 8
SLOT_LANES = 384
COMBINE_TILE = 256
COMBINE_WINDOW = 128
WIN_HI_LANE = 64
POOL_PAD = 32
SCORE_LOOKAHEAD = 5
VMEM_LIMIT = 56 * 1024 * 1024


def _row_tile(n, target):
    t = min(n, target)
    assert n % t == 0
    return t


def _cparams(n_axes):
    return pltpu.CompilerParams(dimension_semantics=("arbitrary",) * n_axes,
                                vmem_limit_bytes=VMEM_LIMIT)


def _dot(a, b):
    return jnp.dot(a, b, preferred_element_type=F32)


def _dot_nt(a, b):
    return lax.dot_general(a, b, (((1,), (1,)), ((), ())), preferred_element_type=F32)


def _inproj_kernel(x_ref, g1_ref, w_ref, wvt_ref, cos_ref, sin_ref, qg_ref, kg_ref, gm_ref,
                   q_ref, k_ref, vt_ref, p_ref):
    x = x_ref[0]
    ms = jnp.mean(x * x, axis=-1, keepdims=True)
    xn = (x * lax.rsqrt(ms + EPS) * g1_ref[...]).astype(BF16)
    cos = cos_ref[...]
    sin = sin_ref[...]
    gm = gm_ref[...]
    tm = x.shape[0]
    lane = lax.broadcasted_iota(jnp.int32, (tm, LANES), 1)
    low_half = (lane & 32) == 0

    def norm_rope(t, ss, g, scale):
        tn = t * lax.rsqrt(ss * (1.0 / HEAD_DIM) + EPS) * g
        sw = jnp.where(low_half, pltpu.roll(tn, LANES - 32, 1), pltpu.roll(tn, 32, 1))
        r = tn * cos + sw * sin
        if scale != 1.0:
            r = r * scale
        return r.astype(BF16)

    pq = _dot(xn, w_ref[:, 0:ATTN_WIDTH])
    pk = _dot(xn, w_ref[:, ATTN_WIDTH:2 * ATTN_WIDTH])
    for h in range(HEADS):
        tq = pq[:, h * LANES:(h + 1) * LANES]
        tk = pk[:, h * LANES:(h + 1) * LANES]
        sq = jnp.concatenate([tq * tq, tk * tk], axis=1)
        hi = sq.astype(BF16)
        lo = (sq - hi.astype(F32)).astype(BF16)
        ss = _dot(hi, gm) + _dot(lo, gm)
        q_ref[0, h] = norm_rope(tq, ss[:, 0:LANES], qg_ref[...], HEAD_DIM ** -0.5 * math.log2(math.e))
        k_ref[0, h] = norm_rope(tk, ss[:, LANES:2 * LANES], kg_ref[...], 1.0)
    pvt = _dot_nt(wvt_ref[...], xn)
    for h in range(HEADS):
        vt_ref[0, h] = pvt[h * VALUE_DIM:(h + 1) * VALUE_DIM, :].astype(BF16)
    p_ref[0] = _dot(xn, w_ref[:, 3 * ATTN_WIDTH:IN_WIDTH])


def _in_proj(x, g1, w_in, wvt, cos, sin, qg, kg, gm, tm):
    nb, n, _ = x.shape
    grid = (nb, n // tm)
    hd = jax.ShapeDtypeStruct((nb, HEADS, n, LANES), BF16)
    hdt = jax.ShapeDtypeStruct((nb, HEADS, VALUE_DIM, n), BF16)
    head_spec = pl.BlockSpec((1, HEADS, tm, LANES), lambda b, i: (b, 0, i, 0))
    headt_spec = pl.BlockSpec((1, HEADS, VALUE_DIM, tm), lambda b, i: (b, 0, 0, i))
    const = lambda shape: pl.BlockSpec(shape, lambda b, i: (0,) * len(shape))
    return pl.pallas_call(
        _inproj_kernel,
        grid=grid,
        in_specs=[
            pl.BlockSpec((1, tm, D_MODEL), lambda b, i: (b, i, 0)),
            const((1, D_MODEL)),
            const((D_MODEL, IN_WIDTH)),
            const((ATTN_WIDTH, D_MODEL)),
            pl.BlockSpec((tm, LANES), lambda b, i: (i, 0)),
            pl.BlockSpec((tm, LANES), lambda b, i: (i, 0)),
            const((1, LANES)),
            const((1, LANES)),
            const((MXU_DIM, MXU_DIM)),
        ],
        out_specs=[head_spec, head_spec, headt_spec,
                   pl.BlockSpec((1, tm, POOL_WIDTH), lambda b, i: (b, i, 0))],
        out_shape=[hd, hd, hdt, jax.ShapeDtypeStruct((nb, n, POOL_WIDTH), F32)],
        compiler_params=_cparams(2),
        name="in_proj",
    )(x, g1, w_in, wvt, cos, sin, qg, kg, gm)


def _attn_kernel(q_ref, qm_ref, kx_ref, km_ref, vxt_ref, vmt_ref, lq1_ref, lk1_ref, lq2_ref, lk2_ref,
                 sg_ref, o_ref, om_ref):
    kx = kx_ref[0, 0]
    km = km_ref[0, 0]
    vxt = vxt_ref[0, 0]
    vmt = vmt_ref[0, 0]

    def split_maps(q):
        lane = lax.broadcasted_iota(jnp.int32, q.shape, 1)
        zero = jnp.zeros_like(q)
        return jnp.where(lane < HEAD_DIM, q, zero), jnp.where(lane >= HEAD_DIM, q, zero)

    def scores(qq):
        n = qq.shape[0]
        meta_valid = lax.broadcasted_iota(jnp.int32, (LANES, n), 0) < N_META
        sx = _dot_nt(kx, qq)
        sm = jnp.where(meta_valid, _dot_nt(km, qq), -1e30)
        m = jnp.maximum(jnp.max(sx, axis=0, keepdims=True), jnp.max(sm, axis=0, keepdims=True))
        return sx, sm, m

    def weighted_values(sx, sm, m):
        px = jnp.exp2(sx - m)
        pm = jnp.exp2(sm - m)
        l = jnp.sum(px, axis=0, keepdims=True) + jnp.sum(pm, axis=0, keepdims=True)
        ot = _dot(vxt, px.astype(BF16)) + _dot(vmt, pm.astype(BF16))
        return ot / l

    q_maps = split_maps(q_ref[0, 0])
    qm_maps = split_maps(qm_ref[0, 0])
    tq = q_ref.shape[2]
    tqs = MXU_DIM if tq % MXU_DIM == 0 else tq
    n_sub = tq // tqs
    units = [(("m", mp), qm_maps[mp]) for mp in range(2)]
    units += [((j, mp), q_maps[mp][j * tqs:(j + 1) * tqs]) for j in range(n_sub) for mp in range(2)]
    outs = {}
    pending = []
    for key, qq in units:
        pending.append((key, scores(qq)))
        if len(pending) > SCORE_LOOKAHEAD:
            k0, args = pending.pop(0)
            outs[k0] = weighted_values(*args)
    for k0, args in pending:
        outs[k0] = weighted_values(*args)

    lam = (jnp.exp(jnp.sum(lq1_ref[...] * lk1_ref[...], axis=-1, keepdims=True))
           - jnp.exp(jnp.sum(lq2_ref[...] * lk2_ref[...], axis=-1, keepdims=True))
           + LAM_INIT)

    def finish(key):
        ot = outs[(key, 0)] - lam * outs[(key, 1)]
        ms = jnp.mean(ot * ot, axis=0, keepdims=True)
        yt = (ot * lax.rsqrt(ms + EPS) * sg_ref[...]) * (1.0 - LAM_INIT)
        return yt.T.astype(BF16)

    om_ref[0] = finish("m")
    for j in range(n_sub):
        o_ref[0, j * tqs:(j + 1) * tqs, :] = finish(j)


def _attention(q, qm, kx, km, vxt, vmt, lams, sg):
    nb, _, s, _ = q.shape
    grid = (nb, HEADS)
    vec = lambda w: pl.BlockSpec((1, w), lambda b, h: (0, 0))
    return pl.pallas_call(
        _attn_kernel,
        grid=grid,
        in_specs=[
            pl.BlockSpec((1, 1, s, LANES), lambda b, h: (b, h, 0, 0)),
            pl.BlockSpec((1, 1, LANES, LANES), lambda b, h: (0, h, 0, 0)),
            pl.BlockSpec((1, 1, s, LANES), lambda b, h: (b, h, 0, 0)),
            pl.BlockSpec((1, 1, LANES, LANES), lambda b, h: (0, h, 0, 0)),
            pl.BlockSpec((1, 1, VALUE_DIM, s), lambda b, h: (b, h, 0, 0)),
            pl.BlockSpec((1, 1, VALUE_DIM, LANES), lambda b, h: (0, h, 0, 0)),
            vec(HEAD_DIM), vec(HEAD_DIM), vec(HEAD_DIM), vec(HEAD_DIM),
            pl.BlockSpec((VALUE_DIM, 1), lambda b, h: (0, 0)),
        ],
        out_specs=[pl.BlockSpec((1, s, VALUE_DIM), lambda b, h: (b, 0, h)),
                   pl.BlockSpec((1, LANES, VALUE_DIM), lambda b, h: (b, 0, h))],
        out_shape=[jax.ShapeDtypeStruct((nb, s, ATTN_WIDTH), BF16),
                   jax.ShapeDtypeStruct((nb, LANES, ATTN_WIDTH), BF16)],
        compiler_params=_cparams(2),
        name="diff_attn",
    )(q, qm, kx, km, vxt, vmt, *lams, sg)


def _pool_kernel(pm_ref, px_ref, pw_ref, ps_ref, ox_ref, om_ref, seq_ref, *, seq_len):
    lp = seq_ref.shape[0]
    s = px_ref.shape[1]
    seq_ref[0:POOL_PAD, :] = jnp.zeros((POOL_PAD, POOL_WIDTH), F32)
    seq_ref[POOL_PAD + seq_len:lp, :] = jnp.zeros((lp - POOL_PAD - seq_len, POOL_WIDTH), F32)
    seq_ref[POOL_PAD:POOL_PAD + N_META, :] = pm_ref[...]
    seq_ref[POOL_PAD + N_META:POOL_PAD + seq_len, :] = px_ref[0]
    t = lax.broadcasted_iota(jnp.int32, (lp, 1), 0) - POOL_PAD
    for g, w in enumerate(POOL_WINDOWS):
        c0, c1 = g * POOL_GROUP_DIM, (g + 1) * POOL_GROUP_DIM
        sg = seq_ref[:, c0:c1]
        win = sg + pltpu.roll(sg, 1, 0)
        span = 1
        while 2 * span < w:
            win = pltpu.roll(win, span, 0) + pltpu.roll(win, lp - span, 0)
            span *= 2
        cnt = jnp.minimum(t + w // 2, seq_len) - jnp.maximum(t - w // 2, 0)
        cnt = jnp.maximum(cnt, 1).astype(F32)
        d = (win / cnt - sg).astype(BF16)
        y = _dot(d, pw_ref[g]) * ps_ref[:, c0:c1]
        om_ref[0, :, c0:c1] = y[POOL_PAD:POOL_PAD + N_META].astype(BF16)
        ox_ref[0, :, c0:c1] = y[POOL_PAD + N_META:POOL_PAD + N_META + s].astype(BF16)


def _pool(p_m, p_x, pool_w, pool_scale):
    nb, s, _ = p_x.shape
    seq_len = s + N_META
    lp = seq_len + 2 * POOL_PAD
    return pl.pallas_call(
        functools.partial(_pool_kernel, seq_len=seq_len),
        grid=(nb,),
        in_specs=[
            pl.BlockSpec((N_META, POOL_WIDTH), lambda b: (0, 0)),
            pl.BlockSpec((1, s, POOL_WIDTH), lambda b: (b, 0, 0)),
            pl.BlockSpec((len(POOL_WINDOWS), POOL_GROUP_DIM, POOL_GROUP_DIM), lambda b: (0, 0, 0)),
            pl.BlockSpec((1, POOL_WIDTH), lambda b: (0, 0)),
        ],
        out_specs=[pl.BlockSpec((1, s, POOL_WIDTH), lambda b: (b, 0, 0)),
                   pl.BlockSpec((1, N_META, POOL_WIDTH), lambda b: (b, 0, 0))],
        out_shape=[jax.ShapeDtypeStruct((nb, s, POOL_WIDTH), BF16),
                   jax.ShapeDtypeStruct((nb, N_META, POOL_WIDTH), BF16)],
        scratch_shapes=[pltpu.VMEM((lp, POOL_WIDTH), F32)],
        compiler_params=_cparams(1),
        name="pool_mixer",
    )(p_m, p_x, pool_w, pool_scale)


def _outproj_kernel(a_ref, p_ref, x_ref, wo_ref, g2_ref, wrh_ref, wrl_ref,
                    h_ref, hn_ref, aff_ref):
    mix = _dot(a_ref[0], wo_ref[0:ATTN_WIDTH, :]) + _dot(p_ref[0], wo_ref[ATTN_WIDTH:, :])
    h = x_ref[0] + mix
    h_ref[0] = h
    ms = jnp.mean(h * h, axis=-1, keepdims=True)
    hn = h * lax.rsqrt(ms + EPS) * g2_ref[...]
    for g in range(hn.shape[0] // 8):
        for j in range(LANE_TILES):
            r0 = (g * LANE_TILES + j) * 8
            hn_ref[0, r0:r0 + 8, :] = hn[g * 8:(g + 1) * 8, j * LANES:(j + 1) * LANES]
    hi = hn.astype(BF16)
    lo = (hn - hi.astype(F32)).astype(BF16)
    wrh = wrh_ref[...]
    logits = _dot_nt(wrh, hi) + _dot_nt(wrl_ref[...], hi) + _dot_nt(wrh, lo)
    m = jnp.max(logits, axis=0, keepdims=True)
    e = jnp.exp(logits - m)
    aff_ref[0] = e / jnp.sum(e, axis=0, keepdims=True)


def _out_proj(attn, pool, x, w_out, g2, wr_hi, wr_lo, tm):
    nb, n, _ = x.shape
    grid = (nb, n // tm)
    const = lambda shape: pl.BlockSpec(shape, lambda b, i: (0,) * len(shape))
    return pl.pallas_call(
        _outproj_kernel,
        grid=grid,
        in_specs=[
            pl.BlockSpec((1, tm, ATTN_WIDTH), lambda b, i: (b, i, 0)),
            pl.BlockSpec((1, tm, POOL_WIDTH), lambda b, i: (b, i, 0)),
            pl.BlockSpec((1, tm, D_MODEL), lambda b, i: (b, i, 0)),
            const((D_MODEL, D_MODEL)),
            const((1, D_MODEL)),
            const((N_EXPERTS, D_MODEL)),
            const((N_EXPERTS, D_MODEL)),
        ],
        out_specs=[pl.BlockSpec((1, tm, D_MODEL), lambda b, i: (b, i, 0)),
                   pl.BlockSpec((1, tm * LANE_TILES, LANES), lambda b, i: (b, i, 0)),
                   pl.BlockSpec((1, N_EXPERTS, tm), lambda b, i: (b, 0, i))],
        out_shape=[jax.ShapeDtypeStruct((nb, n, D_MODEL), F32),
                   jax.ShapeDtypeStruct((nb, n * LANE_TILES, LANES), F32),
                   jax.ShapeDtypeStruct((nb, N_EXPERTS, n), F32)],
        compiler_params=_cparams(2),
        name="out_proj_router",
    )(attn, pool, x, w_out, g2, wr_hi, wr_lo)


def _select_kernel(aff_ref, u_ref, slot_ref, win_ref, *, cap, n_tok):
    aff = aff_ref[...]
    rows, lpad = aff.shape

    def body(i, thr):
        cand = thr | jnp.left_shift(jnp.int32(1), 30 - i)
        cand_f = lax.bitcast_convert_type(cand, F32)
        cnt = jnp.sum((aff >= cand_f).astype(jnp.int32), axis=1, keepdims=True)
        return jnp.where(cnt >= cap, cand, thr)

    thr = lax.fori_loop(0, 31, body, jnp.zeros((rows, 1), jnp.int32))
    thr_f = lax.bitcast_convert_type(thr, F32)
    gt = aff > thr_f
    eq = aff == thr_f
    n_gt = jnp.sum(gt.astype(jnp.int32), axis=1, keepdims=True)
    u = u_ref[...]
    rank_eq = _dot(eq.astype(BF16), u)
    sel = gt | (eq & (rank_eq < (cap - n_gt).astype(F32)))
    pos = _dot(sel.astype(BF16), u)
    slot = jnp.where(sel, pos.astype(jnp.int32), -1)
    slot_ref[...] = slot

    lane = lax.broadcasted_iota(jnp.int32, (rows, lpad), 1)
    out_lane = lax.broadcasted_iota(jnp.int32, (rows, LANES), 1)
    win = jnp.zeros((rows, LANES), jnp.int32)
    n_tiles = n_tok // COMBINE_TILE
    big = jnp.int32(1 << 20)
    for j in range(n_tiles):
        t0 = N_META + j * COMBINE_TILE
        inside = (lane >= t0) & (lane < t0 + COMBINE_TILE) & sel
        lo = jnp.min(jnp.where(inside, slot, big), axis=1, keepdims=True)
        hi = jnp.max(jnp.where(inside, slot, -1), axis=1, keepdims=True)
        first = jnp.where(lo == big, 0, (lo // SLOT_ALIGN) * SLOT_ALIGN)
        win = jnp.where(out_lane == j, first, win)
        win = jnp.where(out_lane == WIN_HI_LANE + j, hi, win)
    win_ref[...] = win


def _select(aff_rows, u, cap, rb, n_tok):
    r, lpad = aff_rows.shape
    assert n_tok // COMBINE_TILE <= WIN_HI_LANE
    return pl.pallas_call(
        functools.partial(_select_kernel, cap=cap, n_tok=n_tok),
        grid=(r // rb,),
        in_specs=[pl.BlockSpec((rb, lpad), lambda i: (i, 0)),
                  pl.BlockSpec((lpad, lpad), lambda i: (0, 0))],
        out_specs=[pl.BlockSpec((rb, lpad), lambda i: (i, 0)),
                   pl.BlockSpec((rb, LANES), lambda i: (i, 0))],
        out_shape=[jax.ShapeDtypeStruct((r, lpad), jnp.int32),
                   jax.ShapeDtypeStruct((r, LANES), jnp.int32)],
        compiler_params=_cparams(1),
        name="expert_choice_select",
    )(aff_rows, u)


def _slot_values(slot_row, value_row):
    s = slot_row.shape[1]
    c = lax.broadcasted_iota(jnp.int32, (SLOT_ROWS, LANES), 0)
    acc = jnp.zeros((SLOT_ROWS, LANES), F32)
    for t0 in range(0, s, LANES):
        hit = slot_row[:, t0:t0 + LANES] == c
        acc = jnp.where(hit, value_row[:, t0:t0 + LANES], acc)
    return jnp.sum(acc, axis=1, keepdims=True)


def _slot_tokens_kernel(slot_ref, idx_ref):
    s = slot_ref.shape[2]
    tok = lax.broadcasted_iota(jnp.int32, (1, s), 1).astype(F32)
    for e in range(N_EXPERTS):
        idx_col = _slot_values(slot_ref[e], tok)
        head = jnp.broadcast_to(idx_col[0:MXU_DIM], (MXU_DIM, LANES)).T[0:1]
        tail = jnp.broadcast_to(idx_col[SLOT_ROWS - LANES:SLOT_ROWS], (LANES, LANES)).T[0:1]
        idx_ref[e, :, 0:MXU_DIM] = head.astype(jnp.int32)
        idx_ref[e, :, MXU_DIM:SLOT_LANES] = pltpu.roll(tail, SLOT_ROWS - MXU_DIM, 1).astype(jnp.int32)


def _slot_tokens(slot3):
    r, _, s = slot3.shape
    nb = r // N_EXPERTS
    rows = pl.BlockSpec((N_EXPERTS, 1, s), lambda b: (b, 0, 0))
    return pl.pallas_call(
        _slot_tokens_kernel,
        grid=(nb,),
        in_specs=[rows],
        out_specs=pl.BlockSpec((N_EXPERTS, 1, SLOT_LANES), lambda b: (b, 0, 0)),
        out_shape=jax.ShapeDtypeStruct((r, 1, SLOT_LANES), jnp.int32),
        compiler_params=_cparams(1),
        name="slot_tokens",
    )(slot3)


def _expert_kernel(idx_ref, slot_ref, aff_ref, hn_ref, wg32_ref, wu32_ref, wd32_ref, y_ref,
                   wg_s, wu_s, wd_s, rows_s):
    p = pl.program_id(0)
    b = pl.program_id(1)
    nb = pl.num_programs(1)
    fill = p % 2
    cur = (p * nb + b) % 2

    def gather_row(c):
        t = idx_ref[0, 0, c]
        first = lax.shift_right_logical(t, 3) * (8 * LANE_TILES) + (t & 7)
        rows_s[1 - cur, pl.ds(c, LANE_TILES, stride=GATHER_STRIDE), :] = hn_ref[0, pl.ds(first, LANE_TILES, stride=8), :]

    def gather_next():
        for c in range(SLOT_ROWS):
            gather_row(c)

    @pl.when(p < N_EXPERTS)
    def _():
        rg = wg32_ref.shape[1]
        rd = wd32_ref.shape[1]
        wg_s[fill, pl.ds(pl.multiple_of(b * rg, rg), rg), :] = wg32_ref[0].astype(BF16)
        wu_s[fill, pl.ds(pl.multiple_of(b * rg, rg), rg), :] = wu32_ref[0].astype(BF16)
        wd_s[fill, pl.ds(pl.multiple_of(b * rd, rd), rd), :] = wd32_ref[0].astype(BF16)

    @pl.when(p == 0)
    def _():
        y_ref[0] = jnp.zeros(y_ref.shape[1:], BF16)

    @pl.when((p == 0) & (b == nb - 1))
    def _():
        def body(c, carry):
            gather_row(c)
            return carry
        lax.fori_loop(0, SLOT_ROWS, body, 0)

    @pl.when(p > 0)
    def _():
        use = 1 - fill
        xg = jnp.concatenate([rows_s[cur, pl.ds(j * GATHER_STRIDE, SLOT_ROWS), :] for j in range(LANE_TILES)],
                             axis=1).astype(BF16)
        gather_next()
        g = _dot(xg, wg_s[use])
        u = _dot(xg, wu_s[use])
        a = (g * jax.nn.sigmoid(g) * u).astype(BF16)
        gate = _slot_values(slot_ref[0], aff_ref[0])
        y_ref[0] = (_dot(a, wd_s[use]) * gate).astype(BF16)


def _experts(idx3, slot3, aff3, hn_words, wg, wu, wd):
    nb = hn_words.shape[0]
    s = slot3.shape[2]
    assert D_MODEL % nb == 0 and (D_MODEL // nb) % 16 == 0
    rg = D_MODEL // nb
    rd = D_FF // nb
    grid = (N_EXPERTS + 1, nb)
    last = N_EXPERTS - 1

    def next_step(p, b):
        t = p * nb + b + 1
        return t % nb, jnp.clip(t // nb - 1, 0, last)

    def next_rows(p, b):
        b1, e1 = next_step(p, b)
        return b1 * N_EXPERTS + e1, 0, 0

    wmap = lambda p, b: (jnp.minimum(p, last), jnp.where(p <= last, b, nb - 1), 0)
    row = pl.BlockSpec((1, 1, s), lambda p, b: (b * N_EXPERTS + jnp.maximum(p - 1, 0), 0, 0))
    return pl.pallas_call(
        _expert_kernel,
        grid=grid,
        in_specs=[
            pl.BlockSpec((1, 1, SLOT_LANES), next_rows, memory_space=pltpu.SMEM),
            row, row,
            pl.BlockSpec((1,) + hn_words.shape[1:], lambda p, b: (next_step(p, b)[0], 0, 0)),
            pl.BlockSpec((1, rg, D_FF), wmap),
            pl.BlockSpec((1, rg, D_FF), wmap),
            pl.BlockSpec((1, rd, D_MODEL), wmap),
        ],
        out_specs=pl.BlockSpec((1, SLOT_ROWS, D_MODEL), lambda p, b: (b, p, 0)),
        out_shape=jax.ShapeDtypeStruct((nb, (N_EXPERTS + 1) * SLOT_ROWS, D_MODEL), BF16),
        scratch_shapes=[pltpu.VMEM((2, D_MODEL, D_FF), BF16),
                        pltpu.VMEM((2, D_MODEL, D_FF), BF16),
                        pltpu.VMEM((2, D_FF, D_MODEL), BF16),
                        pltpu.VMEM((2, GATHER_ROWS, LANES), F32)],
        compiler_params=_cparams(2),
        name="routed_experts",
    )(idx3, slot3, aff3, hn_words, wg, wu, wd)


def _combine_kernel(win_ref, h_ref, st_ref, y_ref, o_ref):
    j = pl.program_id(1)
    st = st_ref[0]
    tm = st.shape[0]
    lane = lax.broadcasted_iota(jnp.int32, (tm, COMBINE_WINDOW), 1)
    firsts = [win_ref[0, e, j] for e in range(N_EXPERTS)]
    n_pass = jnp.int32(1)
    for e in range(N_EXPERTS):
        span = win_ref[0, e, WIN_HI_LANE + j] - firsts[e]
        n_pass = jnp.maximum(n_pass, span // COMBINE_WINDOW + 1)

    per_dot = MXU_DIM // COMBINE_WINDOW

    def one_pass(k, acc):
        for e0 in range(0, N_EXPERTS, per_dot):
            hits, rows = [], []
            for e in range(e0, e0 + per_dot):
                fresh = firsts[e] + k * COMBINE_WINDOW
                start = pl.multiple_of(jnp.minimum(fresh, SLOT_ROWS - COMBINE_WINDOW), SLOT_ALIGN)
                rows.append(y_ref[0, pl.ds(pl.multiple_of((e + 1) * SLOT_ROWS + start, SLOT_ALIGN),
                                           COMBINE_WINDOW), :])
                col = st[:, e:e + 1]
                hits.append(((col == start + lane) & (col >= fresh)).astype(BF16))
            acc = acc + _dot(jnp.concatenate(hits, axis=1), jnp.concatenate(rows, axis=0))
        return acc

    o_ref[0] = lax.fori_loop(0, n_pass, one_pass, h_ref[0])


def _combine(h, slot_t, win, y):
    nb, s, _ = h.shape
    tm = COMBINE_TILE
    return pl.pallas_call(
        _combine_kernel,
        grid=(nb, s // tm),
        in_specs=[
            pl.BlockSpec((1, N_EXPERTS, LANES), lambda b, i: (b, 0, 0), memory_space=pltpu.SMEM),
            pl.BlockSpec((1, tm, D_MODEL), lambda b, i: (b, i, 0)),
            pl.BlockSpec((1, tm, N_EXPERTS), lambda b, i: (b, i, 0)),
            pl.BlockSpec((1, (N_EXPERTS + 1) * SLOT_ROWS, D_MODEL), lambda b, i: (b, 0, 0)),
        ],
        out_specs=pl.BlockSpec((1, tm, D_MODEL), lambda b, i: (b, i, 0)),
        out_shape=jax.ShapeDtypeStruct((nb, s, D_MODEL), F32),
        compiler_params=_cparams(2),
        name="moe_combine",
    )(win, h, slot_t, y)


def _rope_tables(pos):
    half = HEAD_DIM // 2
    freqs = ROPE_THETA ** (-jnp.arange(half, dtype=F32) / half)
    ang = pos[:, None] * freqs[None, :]
    cos = jnp.cos(ang)
    sin = jnp.sin(ang)
    cos_t = jnp.concatenate([cos, cos, cos, cos], axis=1)
    sin_t = jnp.concatenate([-sin, sin, -sin, sin], axis=1)
    return cos_t, sin_t


def kernel(x, meta_tokens, ln1_g, w_in, q_norm_g, k_norm_g, lambda_q1, lambda_k1, lambda_q2, lambda_k2,
           subln_g, pool_w, pool_scale, w_out, ln2_g, w_router, w_gate, w_up, w_down):
    nb, s, _ = x.shape
    seq_len = s + N_META
    cap = CAPACITY_FACTOR * seq_len // N_EXPERTS
    assert cap <= SLOT_ROWS and s % COMBINE_TILE == 0
    l = 0

    w_in_b = w_in[l].astype(BF16)
    w_out_b = w_out[l].astype(BF16)
    pw_b = pool_w[l].astype(BF16)
    wr_t = w_router[l].T
    wr_hi = wr_t.astype(BF16)
    wr_lo = (wr_t - wr_hi.astype(F32)).astype(BF16)
    g1 = ln1_g[l][None]
    g2 = ln2_g[l][None]
    qg = jnp.tile(q_norm_g[l], 2)[None]
    kg = jnp.tile(k_norm_g[l], 2)[None]
    sg = subln_g[l][:, None]
    ps = pool_scale[l][None]
    lams = (lambda_q1[l][None], lambda_k1[l][None], lambda_q2[l][None], lambda_k2[l][None])
    lane = jnp.arange(MXU_DIM)
    gm = (lane[:, None] // HEAD_DIM == lane[None, :] // HEAD_DIM).astype(BF16)
    cos_x, sin_x = _rope_tables(jnp.arange(N_META, seq_len, dtype=F32))
    cos_m, sin_m = _rope_tables(jnp.arange(N_META, dtype=F32))

    tm = _row_tile(s, 512)
    wvt_b = w_in_b[:, 2 * ATTN_WIDTH:3 * ATTN_WIDTH].T
    q_x, k_x, vt_x, p_x = _in_proj(x, g1, w_in_b, wvt_b, cos_x, sin_x, qg, kg, gm, tm)
    q_m, k_m, vt_m, p_m = _in_proj(meta_tokens[None].astype(x.dtype), g1, w_in_b, wvt_b, cos_m, sin_m,
                                   qg, kg, gm, N_META)
    pad_rows = ((0, 0), (0, 0), (0, LANES - N_META), (0, 0))
    q_mp = jnp.pad(q_m, pad_rows)
    k_mp = jnp.pad(k_m, pad_rows)
    vt_mp = jnp.pad(vt_m, ((0, 0), (0, 0), (0, 0), (0, LANES - N_META)))

    attn_x, attn_m = _attention(q_x, q_mp, k_x, k_mp, vt_x, vt_mp, lams, sg)
    attn_m = attn_m[:, :N_META]

    pool_x, pool_m = _pool(p_m[0], p_x, pw_b, ps)

    h_x, hn_x, aff_x = _out_proj(attn_x, pool_x, x, w_out_b, g2, wr_hi, wr_lo, tm)
    n_mrows = nb * N_META
    meta_rows = jnp.broadcast_to(meta_tokens[None].astype(x.dtype), (nb, N_META, D_MODEL)).reshape(1, n_mrows, D_MODEL)
    _, _, aff_m = _out_proj(attn_m.reshape(1, n_mrows, ATTN_WIDTH), pool_m.reshape(1, n_mrows, POOL_WIDTH),
                            meta_rows, w_out_b, g2, wr_hi, wr_lo, n_mrows)
    aff_m = aff_m.reshape(N_EXPERTS, nb, N_META).transpose(1, 0, 2)

    lpad = -(-seq_len // LANES) * LANES
    aff_seq = jnp.concatenate([aff_m, aff_x], axis=2)
    aff_rows = jnp.pad(aff_seq, ((0, 0), (0, 0), (0, lpad - seq_len)), constant_values=-1.0)
    aff_rows = aff_rows.reshape(nb * N_EXPERTS, lpad)
    idx = jnp.arange(lpad)
    upper = (idx[:, None] < idx[None, :]).astype(BF16)
    slot, win = _select(aff_rows, upper, cap, min(nb * N_EXPERTS, 128), s)
    slot_x = slot[:, N_META:seq_len]

    slot3 = slot_x.reshape(nb * N_EXPERTS, 1, s)
    y = _experts(_slot_tokens(slot3), slot3, aff_x.reshape(nb * N_EXPERTS, 1, s),
                 hn_x, w_gate[l], w_up[l], w_down[l])

    slot_t = slot_x.reshape(nb, N_EXPERTS, s).transpose(0, 2, 1)
    return _combine(h_x, slot_t, win.reshape(nb, N_EXPERTS, LANES), y)
```

```python
import functools
import math

import jax
import jax.numpy as jnp
from jax import lax
from jax.experimental import pallas as pl
from jax.experimental.pallas import tpu as pltpu

F32 = jnp.float32
BF16 = jnp.bfloat16

D_MODEL = 1024
N_META = 16
ATTN_WIDTH = 512
POOL_WIDTH = 512
HEAD_DIM = 64
HEADS = 4
VALUE_DIM = 128
POOL_WINDOWS = (2, 4, 8, 16)
POOL_GROUP_DIM = 128
IN_WIDTH = 3 * ATTN_WIDTH + POOL_WIDTH
N_EXPERTS = 16
CAPACITY_FACTOR = 2
D_FF = 2048
ROPE_THETA = 10000.0
EPS = 1e-6
LAM_INIT = 0.8 - 0.6 * math.exp(-0.3 * 0)

LANES = 128
MXU_DIM = 256
LANE_TILES = D_MODEL // LANES
SLOT_ALIGN = 16
SLOT_ROWS = 272
GATHER_STRIDE = SLOT_ROWS + 1
GATHER_ROWS = -(-LANE_TILES * GATHER_STRIDE // 8) * 8
SLOT_LANES = 384
COMBINE_TILE = 512
COMBINE_WINDOW = 128
WIN_HI_LANE = 64
POOL_PAD = 32
SCORE_LOOKAHEAD = 8
VMEM_LIMIT = 56 * 1024 * 1024


def _row_tile(n, target):
    t = min(n, target)
    assert n % t == 0
    return t


def _cparams(n_axes):
    return pltpu.CompilerParams(dimension_semantics=("arbitrary",) * n_axes,
                                vmem_limit_bytes=VMEM_LIMIT)


def _dot(a, b):
    return jnp.dot(a, b, preferred_element_type=F32)


def _dot_nt(a, b):
    return lax.dot_general(a, b, (((1,), (1,)), ((), ())), preferred_element_type=F32)


def _inproj_kernel(x_ref, g1_ref, w_ref, wvt_ref, cos_ref, sin_ref, qg_ref, kg_ref, gm_ref,
                   q_ref, k_ref, vt_ref, p_ref):
    x = x_ref[0]
    ms = jnp.mean(x * x, axis=-1, keepdims=True)
    xn = (x * lax.rsqrt(ms + EPS) * g1_ref[...]).astype(BF16)
    cos = cos_ref[...]
    sin = sin_ref[...]
    gm = gm_ref[...]
    tm = x.shape[0]
    lane = lax.broadcasted_iota(jnp.int32, (tm, LANES), 1)
    low_half = (lane & 32) == 0

    def norm_rope(t, ss, g, scale):
        tn = t * lax.rsqrt(ss * (1.0 / HEAD_DIM) + EPS) * g
        sw = jnp.where(low_half, pltpu.roll(tn, LANES - 32, 1), pltpu.roll(tn, 32, 1))
        r = tn * cos + sw * sin
        if scale != 1.0:
            r = r * scale
        return r.astype(BF16)

    pq = _dot(xn, w_ref[:, 0:ATTN_WIDTH])
    pk = _dot(xn, w_ref[:, ATTN_WIDTH:2 * ATTN_WIDTH])
    for h in range(HEADS):
        tq = pq[:, h * LANES:(h + 1) * LANES]
        tk = pk[:, h * LANES:(h + 1) * LANES]
        sq = jnp.concatenate([tq * tq, tk * tk], axis=1)
        hi = sq.astype(BF16)
        lo = (sq - hi.astype(F32)).astype(BF16)
        ss = _dot(hi, gm) + _dot(lo, gm)
        q_ref[0, h] = norm_rope(tq, ss[:, 0:LANES], qg_ref[...], HEAD_DIM ** -0.5 * math.log2(math.e))
        k_ref[0, h] = norm_rope(tk, ss[:, LANES:2 * LANES], kg_ref[...], 1.0)
    pvt = _dot_nt(wvt_ref[...], xn)
    for h in range(HEADS):
        vt_ref[0, h] = pvt[h * VALUE_DIM:(h + 1) * VALUE_DIM, :].astype(BF16)
    p_ref[0] = _dot(xn, w_ref[:, 3 * ATTN_WIDTH:IN_WIDTH])


def _in_proj(x, g1, w_in, wvt, cos, sin, qg, kg, gm, tm):
    nb, n, _ = x.shape
    grid = (nb, n // tm)
    hd = jax.ShapeDtypeStruct((nb, HEADS, n, LANES), BF16)
    hdt = jax.ShapeDtypeStruct((nb, HEADS, VALUE_DIM, n), BF16)
    head_spec = pl.BlockSpec((1, HEADS, tm, LANES), lambda b, i: (b, 0, i, 0))
    headt_spec = pl.BlockSpec((1, HEADS, VALUE_DIM, tm), lambda b, i: (b, 0, 0, i))
    const = lambda shape: pl.BlockSpec(shape, lambda b, i: (0,) * len(shape))
    return pl.pallas_call(
        _inproj_kernel,
        grid=grid,
        in_specs=[
            pl.BlockSpec((1, tm, D_MODEL), lambda b, i: (b, i, 0)),
            const((1, D_MODEL)),
            const((D_MODEL, IN_WIDTH)),
            const((ATTN_WIDTH, D_MODEL)),
            pl.BlockSpec((tm, LANES), lambda b, i: (i, 0)),
            pl.BlockSpec((tm, LANES), lambda b, i: (i, 0)),
            const((1, LANES)),
            const((1, LANES)),
            const((MXU_DIM, MXU_DIM)),
        ],
        out_specs=[head_spec, head_spec, headt_spec,
                   pl.BlockSpec((1, tm, POOL_WIDTH), lambda b, i: (b, i, 0))],
        out_shape=[hd, hd, hdt, jax.ShapeDtypeStruct((nb, n, POOL_WIDTH), F32)],
        compiler_params=_cparams(2),
        name="in_proj",
    )(x, g1, w_in, wvt, cos, sin, qg, kg, gm)


def _attn_kernel(q_ref, qm_ref, kx_ref, km_ref, vxt_ref, vmt_ref, lq1_ref, lk1_ref, lq2_ref, lk2_ref,
                 sg_ref, o_ref, om_ref):
    kx = kx_ref[0, 0]
    km = km_ref[0, 0]
    vxt = vxt_ref[0, 0]
    vmt = vmt_ref[0, 0]

    def split_maps(q):
        lane = lax.broadcasted_iota(jnp.int32, q.shape, 1)
        zero = jnp.zeros_like(q)
        return jnp.where(lane < HEAD_DIM, q, zero), jnp.where(lane >= HEAD_DIM, q, zero)

    def scores(qq):
        n = qq.shape[0]
        meta_valid = lax.broadcasted_iota(jnp.int32, (LANES, n), 0) < N_META
        sx = _dot_nt(kx, qq)
        sm = jnp.where(meta_valid, _dot_nt(km, qq), -1e30)
        m = jnp.maximum(jnp.max(sx, axis=0, keepdims=True), jnp.max(sm, axis=0, keepdims=True))
        return sx, sm, m

    def weighted_values(sx, sm, m):
        px = jnp.exp2(sx - m)
        pm = jnp.exp2(sm - m)
        l = jnp.sum(px, axis=0, keepdims=True) + jnp.sum(pm, axis=0, keepdims=True)
        ot = _dot(vxt, px.astype(BF16)) + _dot(vmt, pm.astype(BF16))
        return ot / l

    q_maps = split_maps(q_ref[0, 0])
    qm_maps = split_maps(qm_ref[0, 0])
    tq = q_ref.shape[2]
    tqs = MXU_DIM if tq % MXU_DIM == 0 else tq
    n_sub = tq // tqs
    units = [(("m", mp), qm_maps[mp]) for mp in range(2)]
    units += [((j, mp), q_maps[mp][j * tqs:(j + 1) * tqs]) for j in range(n_sub) for mp in range(2)]
    outs = {}
    pending = []
    for key, qq in units:
        pending.append((key, scores(qq)))
        if len(pending) > SCORE_LOOKAHEAD:
            k0, args = pending.pop(0)
            outs[k0] = weighted_values(*args)
    for k0, args in pending:
        outs[k0] = weighted_values(*args)

    lam = (jnp.exp(jnp.sum(lq1_ref[...] * lk1_ref[...], axis=-1, keepdims=True))
           - jnp.exp(jnp.sum(lq2_ref[...] * lk2_ref[...], axis=-1, keepdims=True))
           + LAM_INIT)

    def finish(key):
        ot = outs[(key, 0)] - lam * outs[(key, 1)]
        ms = jnp.mean(ot * ot, axis=0, keepdims=True)
        yt = (ot * lax.rsqrt(ms + EPS) * sg_ref[...]) * (1.0 - LAM_INIT)
        return yt.T.astype(BF16)

    om_ref[0] = finish("m")
    for j in range(n_sub):
        o_ref[0, j * tqs:(j + 1) * tqs, :] = finish(j)


def _attention(q, qm, kx, km, vxt, vmt, lams, sg):
    nb, _, s, _ = q.shape
    grid = (nb, HEADS)
    vec = lambda w: pl.BlockSpec((1, w), lambda b, h: (0, 0))
    return pl.pallas_call(
        _attn_kernel,
        grid=grid,
        in_specs=[
            pl.BlockSpec((1, 1, s, LANES), lambda b, h: (b, h, 0, 0)),
            pl.BlockSpec((1, 1, LANES, LANES), lambda b, h: (0, h, 0, 0)),
            pl.BlockSpec((1, 1, s, LANES), lambda b, h: (b, h, 0, 0)),
            pl.BlockSpec((1, 1, LANES, LANES), lambda b, h: (0, h, 0, 0)),
            pl.BlockSpec((1, 1, VALUE_DIM, s), lambda b, h: (b, h, 0, 0)),
            pl.BlockSpec((1, 1, VALUE_DIM, LANES), lambda b, h: (0, h, 0, 0)),
            vec(HEAD_DIM), vec(HEAD_DIM), vec(HEAD_DIM), vec(HEAD_DIM),
            pl.BlockSpec((VALUE_DIM, 1), lambda b, h: (0, 0)),
        ],
        out_specs=[pl.BlockSpec((1, s, VALUE_DIM), lambda b, h: (b, 0, h)),
                   pl.BlockSpec((1, LANES, VALUE_DIM), lambda b, h: (b, 0, h))],
        out_shape=[jax.ShapeDtypeStruct((nb, s, ATTN_WIDTH), BF16),
                   jax.ShapeDtypeStruct((nb, LANES, ATTN_WIDTH), BF16)],
        compiler_params=_cparams(2),
        name="diff_attn",
    )(q, qm, kx, km, vxt, vmt, *lams, sg)


def _pool_kernel(pm_ref, px_ref, pw_ref, ps_ref, ox_ref, om_ref, seq_ref, *, seq_len):
    lp = seq_ref.shape[0]
    s = px_ref.shape[1]
    seq_ref[0:POOL_PAD, :] = jnp.zeros((POOL_PAD, POOL_WIDTH), F32)
    seq_ref[POOL_PAD + seq_len:lp, :] = jnp.zeros((lp - POOL_PAD - seq_len, POOL_WIDTH), F32)
    seq_ref[POOL_PAD:POOL_PAD + N_META, :] = pm_ref[...]
    seq_ref[POOL_PAD + N_META:POOL_PAD + seq_len, :] = px_ref[0]
    t = lax.broadcasted_iota(jnp.int32, (lp, 1), 0) - POOL_PAD
    for g, w in enumerate(POOL_WINDOWS):
        c0, c1 = g * POOL_GROUP_DIM, (g + 1) * POOL_GROUP_DIM
        sg = seq_ref[:, c0:c1]
        win = sg + pltpu.roll(sg, 1, 0)
        span = 1
        while 2 * span < w:
            win = pltpu.roll(win, span, 0) + pltpu.roll(win, lp - span, 0)
            span *= 2
        cnt = jnp.minimum(t + w // 2, seq_len) - jnp.maximum(t - w // 2, 0)
        cnt = jnp.maximum(cnt, 1).astype(F32)
        d = (win / cnt - sg).astype(BF16)
        y = _dot(d, pw_ref[g]) * ps_ref[:, c0:c1]
        om_ref[0, :, c0:c1] = y[POOL_PAD:POOL_PAD + N_META].astype(BF16)
        ox_ref[0, :, c0:c1] = y[POOL_PAD + N_META:POOL_PAD + N_META + s].astype(BF16)


def _pool(p_m, p_x, pool_w, pool_scale):
    nb, s, _ = p_x.shape
    seq_len = s + N_META
    lp = seq_len + 2 * POOL_PAD
    return pl.pallas_call(
        functools.partial(_pool_kernel, seq_len=seq_len),
        grid=(nb,),
        in_specs=[
            pl.BlockSpec((N_META, POOL_WIDTH), lambda b: (0, 0)),
            pl.BlockSpec((1, s, POOL_WIDTH), lambda b: (b, 0, 0)),
            pl.BlockSpec((len(POOL_WINDOWS), POOL_GROUP_DIM, POOL_GROUP_DIM), lambda b: (0, 0, 0)),
            pl.BlockSpec((1, POOL_WIDTH), lambda b: (0, 0)),
        ],
        out_specs=[pl.BlockSpec((1, s, POOL_WIDTH), lambda b: (b, 0, 0)),
                   pl.BlockSpec((1, N_META, POOL_WIDTH), lambda b: (b, 0, 0))],
        out_shape=[jax.ShapeDtypeStruct((nb, s, POOL_WIDTH), BF16),
                   jax.ShapeDtypeStruct((nb, N_META, POOL_WIDTH), BF16)],
        scratch_shapes=[pltpu.VMEM((lp, POOL_WIDTH), F32)],
        compiler_params=_cparams(1),
        name="pool_mixer",
    )(p_m, p_x, pool_w, pool_scale)


def _outproj_kernel(a_ref, p_ref, x_ref, wo_ref, g2_ref, wrh_ref, wrl_ref,
                    h_ref, hn_ref, aff_ref):
    mix = _dot(a_ref[0], wo_ref[0:ATTN_WIDTH, :]) + _dot(p_ref[0], wo_ref[ATTN_WIDTH:, :])
    h = x_ref[0] + mix
    h_ref[0] = h
    ms = jnp.mean(h * h, axis=-1, keepdims=True)
    hn = h * lax.rsqrt(ms + EPS) * g2_ref[...]
    for g in range(hn.shape[0] // 8):
        for j in range(LANE_TILES):
            r0 = (g * LANE_TILES + j) * 8
            hn_ref[0, r0:r0 + 8, :] = hn[g * 8:(g + 1) * 8, j * LANES:(j + 1) * LANES]
    hi = hn.astype(BF16)
    lo = (hn - hi.astype(F32)).astype(BF16)
    wrh = wrh_ref[...]
    logits = _dot_nt(wrh, hi) + _dot_nt(wrl_ref[...], hi) + _dot_nt(wrh, lo)
    m = jnp.max(logits, axis=0, keepdims=True)
    e = jnp.exp(logits - m)
    aff_ref[0] = e / jnp.sum(e, axis=0, keepdims=True)


def _out_proj(attn, pool, x, w_out, g2, wr_hi, wr_lo, tm):
    nb, n, _ = x.shape
    grid = (nb, n // tm)
    const = lambda shape: pl.BlockSpec(shape, lambda b, i: (0,) * len(shape))
    return pl.pallas_call(
        _outproj_kernel,
        grid=grid,
        in_specs=[
            pl.BlockSpec((1, tm, ATTN_WIDTH), lambda b, i: (b, i, 0)),
            pl.BlockSpec((1, tm, POOL_WIDTH), lambda b, i: (b, i, 0)),
            pl.BlockSpec((1, tm, D_MODEL), lambda b, i: (b, i, 0)),
            const((D_MODEL, D_MODEL)),
            const((1, D_MODEL)),
            const((N_EXPERTS, D_MODEL)),
            const((N_EXPERTS, D_MODEL)),
        ],
        out_specs=[pl.BlockSpec((1, tm, D_MODEL), lambda b, i: (b, i, 0)),
                   pl.BlockSpec((1, tm * LANE_TILES, LANES), lambda b, i: (b, i, 0)),
                   pl.BlockSpec((1, N_EXPERTS, tm), lambda b, i: (b, 0, i))],
        out_shape=[jax.ShapeDtypeStruct((nb, n, D_MODEL), F32),
                   jax.ShapeDtypeStruct((nb, n * LANE_TILES, LANES), F32),
                   jax.ShapeDtypeStruct((nb, N_EXPERTS, n), F32)],
        compiler_params=_cparams(2),
        name="out_proj_router",
    )(attn, pool, x, w_out, g2, wr_hi, wr_lo)


def _select_kernel(aff_ref, u_ref, slot_ref, win_ref, *, cap, n_tok):
    aff = aff_ref[...]
    rows, lpad = aff.shape

    def body(i, thr):
        cand = thr | jnp.left_shift(jnp.int32(1), 30 - i)
        cand_f = lax.bitcast_convert_type(cand, F32)
        cnt = jnp.sum((aff >= cand_f).astype(jnp.int32), axis=1, keepdims=True)
        return jnp.where(cnt >= cap, cand, thr)

    thr = lax.fori_loop(0, 31, body, jnp.zeros((rows, 1), jnp.int32))
    thr_f = lax.bitcast_convert_type(thr, F32)
    gt = aff > thr_f
    eq = aff == thr_f
    n_gt = jnp.sum(gt.astype(jnp.int32), axis=1, keepdims=True)
    u = u_ref[...]
    rank_eq = _dot(eq.astype(BF16), u)
    sel = gt | (eq & (rank_eq < (cap - n_gt).astype(F32)))
    pos = _dot(sel.astype(BF16), u)
    slot = jnp.where(sel, pos.astype(jnp.int32), -1)
    slot_ref[...] = slot

    lane = lax.broadcasted_iota(jnp.int32, (rows, lpad), 1)
    out_lane = lax.broadcasted_iota(jnp.int32, (rows, LANES), 1)
    win = jnp.zeros((rows, LANES), jnp.int32)
    n_tiles = n_tok // COMBINE_TILE
    big = jnp.int32(1 << 20)
    for j in range(n_tiles):
        t0 = N_META + j * COMBINE_TILE
        inside = (lane >= t0) & (lane < t0 + COMBINE_TILE) & sel
        lo = jnp.min(jnp.where(inside, slot, big), axis=1, keepdims=True)
        hi = jnp.max(jnp.where(inside, slot, -1), axis=1, keepdims=True)
        first = jnp.where(lo == big, 0, (lo // SLOT_ALIGN) * SLOT_ALIGN)
        win = jnp.where(out_lane == j, first, win)
        win = jnp.where(out_lane == WIN_HI_LANE + j, hi, win)
    win_ref[...] = win


def _select(aff_rows, u, cap, rb, n_tok):
    r, lpad = aff_rows.shape
    assert n_tok // COMBINE_TILE <= WIN_HI_LANE
    return pl.pallas_call(
        functools.partial(_select_kernel, cap=cap, n_tok=n_tok),
        grid=(r // rb,),
        in_specs=[pl.BlockSpec((rb, lpad), lambda i: (i, 0)),
                  pl.BlockSpec((lpad, lpad), lambda i: (0, 0))],
        out_specs=[pl.BlockSpec((rb, lpad), lambda i: (i, 0)),
                   pl.BlockSpec((rb, LANES), lambda i: (i, 0))],
        out_shape=[jax.ShapeDtypeStruct((r, lpad), jnp.int32),
                   jax.ShapeDtypeStruct((r, LANES), jnp.int32)],
        compiler_params=_cparams(1),
        name="expert_choice_select",
    )(aff_rows, u)


def _slot_values(slot_row, value_row):
    s = slot_row.shape[1]
    c = lax.broadcasted_iota(jnp.int32, (SLOT_ROWS, LANES), 0)
    acc = jnp.zeros((SLOT_ROWS, LANES), F32)
    for t0 in range(0, s, LANES):
        hit = slot_row[:, t0:t0 + LANES] == c
        acc = jnp.where(hit, value_row[:, t0:t0 + LANES], acc)
    return jnp.sum(acc, axis=1, keepdims=True)


def _slot_tokens_kernel(slot_ref, idx_ref):
    s = slot_ref.shape[2]
    tok = lax.broadcasted_iota(jnp.int32, (1, s), 1).astype(F32)
    for e in range(N_EXPERTS):
        idx_col = _slot_values(slot_ref[e], tok)
        head = jnp.broadcast_to(idx_col[0:MXU_DIM], (MXU_DIM, LANES)).T[0:1]
        tail = jnp.broadcast_to(idx_col[SLOT_ROWS - LANES:SLOT_ROWS], (LANES, LANES)).T[0:1]
        idx_ref[e, :, 0:MXU_DIM] = head.astype(jnp.int32)
        idx_ref[e, :, MXU_DIM:SLOT_LANES] = pltpu.roll(tail, SLOT_ROWS - MXU_DIM, 1).astype(jnp.int32)


def _slot_tokens(slot3):
    r, _, s = slot3.shape
    nb = r // N_EXPERTS
    rows = pl.BlockSpec((N_EXPERTS, 1, s), lambda b: (b, 0, 0))
    return pl.pallas_call(
        _slot_tokens_kernel,
        grid=(nb,),
        in_specs=[rows],
        out_specs=pl.BlockSpec((N_EXPERTS, 1, SLOT_LANES), lambda b: (b, 0, 0)),
        out_shape=jax.ShapeDtypeStruct((r, 1, SLOT_LANES), jnp.int32),
        compiler_params=_cparams(1),
        name="slot_tokens",
    )(slot3)


def _expert_kernel(idx_ref, slot_ref, aff_ref, hn_ref, wg32_ref, wu32_ref, wd32_ref, y_ref,
                   wg_s, wu_s, wd_s, rows_s):
    p = pl.program_id(0)
    b = pl.program_id(1)
    nb = pl.num_programs(1)
    fill = p % 2
    cur = (p * nb + b) % 2

    def gather_row(c):
        t = idx_ref[0, 0, c]
        first = lax.shift_right_logical(t, 3) * (8 * LANE_TILES) + (t & 7)
        rows_s[1 - cur, pl.ds(c, LANE_TILES, stride=GATHER_STRIDE), :] = hn_ref[0, pl.ds(first, LANE_TILES, stride=8), :]

    def gather_next():
        for c in range(SLOT_ROWS):
            gather_row(c)

    @pl.when(p < N_EXPERTS)
    def _():
        rg = wg32_ref.shape[1]
        rd = wd32_ref.shape[1]
        wg_s[fill, pl.ds(pl.multiple_of(b * rg, rg), rg), :] = wg32_ref[0].astype(BF16)
        wu_s[fill, pl.ds(pl.multiple_of(b * rg, rg), rg), :] = wu32_ref[0].astype(BF16)
        wd_s[fill, pl.ds(pl.multiple_of(b * rd, rd), rd), :] = wd32_ref[0].astype(BF16)

    @pl.when(p == 0)
    def _():
        y_ref[0] = jnp.zeros(y_ref.shape[1:], BF16)

    @pl.when((p == 0) & (b == nb - 1))
    def _():
        def body(c, carry):
            gather_row(c)
            return carry
        lax.fori_loop(0, SLOT_ROWS, body, 0)

    @pl.when(p > 0)
    def _():
        use = 1 - fill
        xg = jnp.concatenate([rows_s[cur, pl.ds(j * GATHER_STRIDE, SLOT_ROWS), :] for j in range(LANE_TILES)],
                             axis=1).astype(BF16)
        gather_next()
        g = _dot(xg, wg_s[use])
        u = _dot(xg, wu_s[use])
        a = (g * jax.nn.sigmoid(g) * u).astype(BF16)
        gate = _slot_values(slot_ref[0], aff_ref[0])
        y_ref[0] = (_dot(a, wd_s[use]) * gate).astype(BF16)


def _experts(idx3, slot3, aff3, hn_words, wg, wu, wd):
    nb = hn_words.shape[0]
    s = slot3.shape[2]
    assert D_MODEL % nb == 0 and (D_MODEL // nb) % 16 == 0
    rg = D_MODEL // nb
    rd = D_FF // nb
    grid = (N_EXPERTS + 1, nb)
    last = N_EXPERTS - 1

    def next_step(p, b):
        t = p * nb + b + 1
        return t % nb, jnp.clip(t // nb - 1, 0, last)

    def next_rows(p, b):
        b1, e1 = next_step(p, b)
        return b1 * N_EXPERTS + e1, 0, 0

    wmap = lambda p, b: (jnp.minimum(p, last), jnp.where(p <= last, b, nb - 1), 0)
    row = pl.BlockSpec((1, 1, s), lambda p, b: (b * N_EXPERTS + jnp.maximum(p - 1, 0), 0, 0))
    return pl.pallas_call(
        _expert_kernel,
        grid=grid,
        in_specs=[
            pl.BlockSpec((1, 1, SLOT_LANES), next_rows, memory_space=pltpu.SMEM),
            row, row,
            pl.BlockSpec((1,) + hn_words.shape[1:], lambda p, b: (next_step(p, b)[0], 0, 0)),
            pl.BlockSpec((1, rg, D_FF), wmap),
            pl.BlockSpec((1, rg, D_FF), wmap),
            pl.BlockSpec((1, rd, D_MODEL), wmap),
        ],
        out_specs=pl.BlockSpec((1, SLOT_ROWS, D_MODEL), lambda p, b: (b, p, 0)),
        out_shape=jax.ShapeDtypeStruct((nb, (N_EXPERTS + 1) * SLOT_ROWS, D_MODEL), BF16),
        scratch_shapes=[pltpu.VMEM((2, D_MODEL, D_FF), BF16),
                        pltpu.VMEM((2, D_MODEL, D_FF), BF16),
                        pltpu.VMEM((2, D_FF, D_MODEL), BF16),
                        pltpu.VMEM((2, GATHER_ROWS, LANES), F32)],
        compiler_params=_cparams(2),
        name="routed_experts",
    )(idx3, slot3, aff3, hn_words, wg, wu, wd)


def _combine_kernel(win_ref, h_ref, st_ref, y_ref, o_ref):
    j = pl.program_id(1)
    st = st_ref[0]
    tm = st.shape[0]
    lane = lax.broadcasted_iota(jnp.int32, (tm, COMBINE_WINDOW), 1)
    firsts = [win_ref[0, e, j] for e in range(N_EXPERTS)]
    n_pass = jnp.int32(1)
    for e in range(N_EXPERTS):
        span = win_ref[0, e, WIN_HI_LANE + j] - firsts[e]
        n_pass = jnp.maximum(n_pass, span // COMBINE_WINDOW + 1)

    per_dot = MXU_DIM // COMBINE_WINDOW

    def one_pass(k, acc):
        for e0 in range(0, N_EXPERTS, per_dot):
            hits, rows = [], []
            for e in range(e0, e0 + per_dot):
                fresh = firsts[e] + k * COMBINE_WINDOW
                start = pl.multiple_of(jnp.minimum(fresh, SLOT_ROWS - COMBINE_WINDOW), SLOT_ALIGN)
                rows.append(y_ref[0, pl.ds(pl.multiple_of((e + 1) * SLOT_ROWS + start, SLOT_ALIGN),
                                           COMBINE_WINDOW), :])
                col = st[:, e:e + 1]
                hits.append(((col == start + lane) & (col >= fresh)).astype(BF16))
            acc = acc + _dot(jnp.concatenate(hits, axis=1), jnp.concatenate(rows, axis=0))
        return acc

    o_ref[0] = lax.fori_loop(0, n_pass, one_pass, h_ref[0])


def _combine(h, slot_t, win, y):
    nb, s, _ = h.shape
    tm = COMBINE_TILE
    return pl.pallas_call(
        _combine_kernel,
        grid=(nb, s // tm),
        in_specs=[
            pl.BlockSpec((1, N_EXPERTS, LANES), lambda b, i: (b, 0, 0), memory_space=pltpu.SMEM),
            pl.BlockSpec((1, tm, D_MODEL), lambda b, i: (b, i, 0)),
            pl.BlockSpec((1, tm, N_EXPERTS), lambda b, i: (b, i, 0)),
            pl.BlockSpec((1, (N_EXPERTS + 1) * SLOT_ROWS, D_MODEL), lambda b, i: (b, 0, 0)),
        ],
        out_specs=pl.BlockSpec((1, tm, D_MODEL), lambda b, i: (b, i, 0)),
        out_shape=jax.ShapeDtypeStruct((nb, s, D_MODEL), F32),
        compiler_params=_cparams(2),
        name="moe_combine",
    )(win, h, slot_t, y)


def _rope_tables(pos):
    half = HEAD_DIM // 2
    freqs = ROPE_THETA ** (-jnp.arange(half, dtype=F32) / half)
    ang = pos[:, None] * freqs[None, :]
    cos = jnp.cos(ang)
    sin = jnp.sin(ang)
    cos_t = jnp.concatenate([cos, cos, cos, cos], axis=1)
    sin_t = jnp.concatenate([-sin, sin, -sin, sin], axis=1)
    return cos_t, sin_t


def kernel(x, meta_tokens, ln1_g, w_in, q_norm_g, k_norm_g, lambda_q1, lambda_k1, lambda_q2, lambda_k2,
           subln_g, pool_w, pool_scale, w_out, ln2_g, w_router, w_gate, w_up, w_down):
    nb, s, _ = x.shape
    seq_len = s + N_META
    cap = CAPACITY_FACTOR * seq_len // N_EXPERTS
    assert cap <= SLOT_ROWS and s % COMBINE_TILE == 0
    l = 0

    w_in_b = w_in[l].astype(BF16)
    w_out_b = w_out[l].astype(BF16)
    pw_b = pool_w[l].astype(BF16)
    wr_t = w_router[l].T
    wr_hi = wr_t.astype(BF16)
    wr_lo = (wr_t - wr_hi.astype(F32)).astype(BF16)
    g1 = ln1_g[l][None]
    g2 = ln2_g[l][None]
    qg = jnp.tile(q_norm_g[l], 2)[None]
    kg = jnp.tile(k_norm_g[l], 2)[None]
    sg = subln_g[l][:, None]
    ps = pool_scale[l][None]
    lams = (lambda_q1[l][None], lambda_k1[l][None], lambda_q2[l][None], lambda_k2[l][None])
    lane = jnp.arange(MXU_DIM)
    gm = (lane[:, None] // HEAD_DIM == lane[None, :] // HEAD_DIM).astype(BF16)
    cos_x, sin_x = _rope_tables(jnp.arange(N_META, seq_len, dtype=F32))
    cos_m, sin_m = _rope_tables(jnp.arange(N_META, dtype=F32))

    tm = _row_tile(s, 512)
    wvt_b = w_in_b[:, 2 * ATTN_WIDTH:3 * ATTN_WIDTH].T
    q_x, k_x, vt_x, p_x = _in_proj(x, g1, w_in_b, wvt_b, cos_x, sin_x, qg, kg, gm, tm)
    q_m, k_m, vt_m, p_m = _in_proj(meta_tokens[None].astype(x.dtype), g1, w_in_b, wvt_b, cos_m, sin_m,
                                   qg, kg, gm, N_META)
    pad_rows = ((0, 0), (0, 0), (0, LANES - N_META), (0, 0))
    q_mp = jnp.pad(q_m, pad_rows)
    k_mp = jnp.pad(k_m, pad_rows)
    vt_mp = jnp.pad(vt_m, ((0, 0), (0, 0), (0, 0), (0, LANES - N_META)))

    attn_x, attn_m = _attention(q_x, q_mp, k_x, k_mp, vt_x, vt_mp, lams, sg)
    attn_m = attn_m[:, :N_META]

    pool_x, pool_m = _pool(p_m[0], p_x, pw_b, ps)

    h_x, hn_x, aff_x = _out_proj(attn_x, pool_x, x, w_out_b, g2, wr_hi, wr_lo, tm)
    n_mrows = nb * N_META
    meta_rows = jnp.broadcast_to(meta_tokens[None].astype(x.dtype), (nb, N_META, D_MODEL)).reshape(1, n_mrows, D_MODEL)
    _, _, aff_m = _out_proj(attn_m.reshape(1, n_mrows, ATTN_WIDTH), pool_m.reshape(1, n_mrows, POOL_WIDTH),
                            meta_rows, w_out_b, g2, wr_hi, wr_lo, n_mrows)
    aff_m = aff_m.reshape(N_EXPERTS, nb, N_META).transpose(1, 0, 2)

    lpad = -(-seq_len // LANES) * LANES
    aff_seq = jnp.concatenate([aff_m, aff_x], axis=2)
    aff_rows = jnp.pad(aff_seq, ((0, 0), (0, 0), (0, lpad - seq_len)), constant_values=-1.0)
    aff_rows = aff_rows.reshape(nb * N_EXPERTS, lpad)
    idx = jnp.arange(lpad)
    upper = (idx[:, None] < idx[None, :]).astype(BF16)
    slot, win = _select(aff_rows, upper, cap, min(nb * N_EXPERTS, 128), s)
    slot_x = slot[:, N_META:seq_len]

    slot3 = slot_x.reshape(nb * N_EXPERTS, 1, s)
    y = _experts(_slot_tokens(slot3), slot3, aff_x.reshape(nb * N_EXPERTS, 1, s),
                 hn_x, w_gate[l], w_up[l], w_down[l])

    slot_t = slot_x.reshape(nb, N_EXPERTS, s).transpose(0, 2, 1)
    return _combine(h_x, slot_t, win.reshape(nb, N_EXPERTS, LANES), y)
```

```python
import functools
import math

import jax
import jax.numpy as jnp
from jax import lax
from jax.experimental import pallas as pl
from jax.experimental.pallas import tpu as pltpu

F32 = jnp.float32
BF16 = jnp.bfloat16

D_MODEL = 1024
N_META = 16
ATTN_WIDTH = 512
POOL_WIDTH = 512
HEAD_DIM = 64
HEADS = 4
VALUE_DIM = 128
POOL_WINDOWS = (2, 4, 8, 16)
POOL_GROUP_DIM = 128
IN_WIDTH = 3 * ATTN_WIDTH + POOL_WIDTH
N_EXPERTS = 16
CAPACITY_FACTOR = 2
D_FF = 2048
ROPE_THETA = 10000.0
EPS = 1e-6
LAM_INIT = 0.8 - 0.6 * math.exp(-0.3 * 0)

LANES = 128
MXU_DIM = 256
LANE_TILES = D_MODEL // LANES
SLOT_ALIGN = 16
SLOT_ROWS = 272
GATHER_STRIDE = SLOT_ROWS + 1
GATHER_ROWS = -(-LANE_TILES * GATHER_STRIDE // 8) * 8
SLOT_LANES = 384
COMBINE_TILE = 512
COMBINE_WINDOW = 128
WIN_HI_LANE = 64
POOL_PAD = 32
SCORE_LOOKAHEAD = 8
VMEM_LIMIT = 56 * 1024 * 1024


def _row_tile(n, target):
    t = min(n, target)
    assert n % t == 0
    return t


def _cparams(n_axes):
    return pltpu.CompilerParams(dimension_semantics=("arbitrary",) * n_axes,
                                vmem_limit_bytes=VMEM_LIMIT)


def _dot(a, b):
    return jnp.dot(a, b, preferred_element_type=F32)


def _dot_nt(a, b):
    return lax.dot_general(a, b, (((1,), (1,)), ((), ())), preferred_element_type=F32)


def _inproj_kernel(x_ref, g1_ref, w_ref, wvt_ref, cos_ref, sin_ref, qg_ref, kg_ref, gm_ref,
                   q_ref, k_ref, vt_ref, p_ref):
    x = x_ref[0]
    ms = jnp.mean(x * x, axis=-1, keepdims=True)
    xn = (x * lax.rsqrt(ms + EPS) * g1_ref[...]).astype(BF16)
    cos = cos_ref[...]
    sin = sin_ref[...]
    gm = gm_ref[...]
    tm = x.shape[0]
    lane = lax.broadcasted_iota(jnp.int32, (tm, LANES), 1)
    low_half = (lane & 32) == 0

    def norm_rope(t, ss, g, scale):
        tn = t * lax.rsqrt(ss * (1.0 / HEAD_DIM) + EPS) * g
        sw = jnp.where(low_half, pltpu.roll(tn, LANES - 32, 1), pltpu.roll(tn, 32, 1))
        r = tn * cos + sw * sin
        if scale != 1.0:
            r = r * scale
        return r.astype(BF16)

    pq = _dot(xn, w_ref[:, 0:ATTN_WIDTH])
    pk = _dot(xn, w_ref[:, ATTN_WIDTH:2 * ATTN_WIDTH])
    for h in range(HEADS):
        tq = pq[:, h * LANES:(h + 1) * LANES]
        tk = pk[:, h * LANES:(h + 1) * LANES]
        sq = jnp.concatenate([tq * tq, tk * tk], axis=1)
        hi = sq.astype(BF16)
        lo = (sq - hi.astype(F32)).astype(BF16)
        ss = _dot(hi, gm) + _dot(lo, gm)
        q_ref[0, h] = norm_rope(tq, ss[:, 0:LANES], qg_ref[...], HEAD_DIM ** -0.5 * math.log2(math.e))
        k_ref[0, h] = norm_rope(tk, ss[:, LANES:2 * LANES], kg_ref[...], 1.0)
    pvt = _dot_nt(wvt_ref[...], xn)
    for h in range(HEADS):
        vt_ref[0, h] = pvt[h * VALUE_DIM:(h + 1) * VALUE_DIM, :].astype(BF16)
    p_ref[0] = _dot(xn, w_ref[:, 3 * ATTN_WIDTH:IN_WIDTH])


def _in_proj(x, g1, w_in, wvt, cos, sin, qg, kg, gm, tm):
    nb, n, _ = x.shape
    grid = (nb, n // tm)
    hd = jax.ShapeDtypeStruct((nb, HEADS, n, LANES), BF16)
    hdt = jax.ShapeDtypeStruct((nb, HEADS, VALUE_DIM, n), BF16)
    head_spec = pl.BlockSpec((1, HEADS, tm, LANES), lambda b, i: (b, 0, i, 0))
    headt_spec = pl.BlockSpec((1, HEADS, VALUE_DIM, tm), lambda b, i: (b, 0, 0, i))
    const = lambda shape: pl.BlockSpec(shape, lambda b, i: (0,) * len(shape))
    return pl.pallas_call(
        _inproj_kernel,
        grid=grid,
        in_specs=[
            pl.BlockSpec((1, tm, D_MODEL), lambda b, i: (b, i, 0)),
            const((1, D_MODEL)),
            const((D_MODEL, IN_WIDTH)),
            const((ATTN_WIDTH, D_MODEL)),
            pl.BlockSpec((tm, LANES), lambda b, i: (i, 0)),
            pl.BlockSpec((tm, LANES), lambda b, i: (i, 0)),
            const((1, LANES)),
            const((1, LANES)),
            const((MXU_DIM, MXU_DIM)),
        ],
        out_specs=[head_spec, head_spec, headt_spec,
                   pl.BlockSpec((1, tm, POOL_WIDTH), lambda b, i: (b, i, 0))],
        out_shape=[hd, hd, hdt, jax.ShapeDtypeStruct((nb, n, POOL_WIDTH), F32)],
        compiler_params=_cparams(2),
        name="in_proj",
    )(x, g1, w_in, wvt, cos, sin, qg, kg, gm)


def _attn_kernel(q_ref, qm_ref, kx_ref, km_ref, vxt_ref, vmt_ref, lq1_ref, lk1_ref, lq2_ref, lk2_ref,
                 sg_ref, o_ref, om_ref):
    kx = kx_ref[0, 0]
    km = km_ref[0, 0]
    vxt = vxt_ref[0, 0]
    vmt = vmt_ref[0, 0]

    def split_maps(q):
        lane = lax.broadcasted_iota(jnp.int32, q.shape, 1)
        zero = jnp.zeros_like(q)
        return jnp.where(lane < HEAD_DIM, q, zero), jnp.where(lane >= HEAD_DIM, q, zero)

    def scores(qq):
        n = qq.shape[0]
        meta_valid = lax.broadcasted_iota(jnp.int32, (LANES, n), 0) < N_META
        sx = _dot_nt(kx, qq)
        sm = jnp.where(meta_valid, _dot_nt(km, qq), -1e30)
        m = jnp.maximum(jnp.max(sx, axis=0, keepdims=True), jnp.max(sm, axis=0, keepdims=True))
        return sx, sm, m

    def weighted_values(sx, sm, m):
        px = jnp.exp2(sx - m)
        pm = jnp.exp2(sm - m)
        l = jnp.sum(px, axis=0, keepdims=True) + jnp.sum(pm, axis=0, keepdims=True)
        ot = _dot(vxt, px.astype(BF16)) + _dot(vmt, pm.astype(BF16))
        return ot / l

    q_maps = split_maps(q_ref[0, 0])
    qm_maps = split_maps(qm_ref[0, 0])
    tq = q_ref.shape[2]
    tqs = MXU_DIM if tq % MXU_DIM == 0 else tq
    n_sub = tq // tqs
    units = [(("m", mp), qm_maps[mp]) for mp in range(2)]
    units += [((j, mp), q_maps[mp][j * tqs:(j + 1) * tqs]) for j in range(n_sub) for mp in range(2)]
    outs = {}
    pending = []
    for key, qq in units:
        pending.append((key, scores(qq)))
        if len(pending) > SCORE_LOOKAHEAD:
            k0, args = pending.pop(0)
            outs[k0] = weighted_values(*args)
    for k0, args in pending:
        outs[k0] = weighted_values(*args)

    lam = (jnp.exp(jnp.sum(lq1_ref[...] * lk1_ref[...], axis=-1, keepdims=True))
           - jnp.exp(jnp.sum(lq2_ref[...] * lk2_ref[...], axis=-1, keepdims=True))
           + LAM_INIT)

    def finish(key):
        ot = outs[(key, 0)] - lam * outs[(key, 1)]
        ms = jnp.mean(ot * ot, axis=0, keepdims=True)
        yt = (ot * lax.rsqrt(ms + EPS) * sg_ref[...]) * (1.0 - LAM_INIT)
        return yt.T.astype(BF16)

    om_ref[0] = finish("m")
    for j in range(n_sub):
        o_ref[0, j * tqs:(j + 1) * tqs, :] = finish(j)


def _attention(q, qm, kx, km, vxt, vmt, lams, sg):
    nb, _, s, _ = q.shape
    grid = (nb, HEADS)
    vec = lambda w: pl.BlockSpec((1, w), lambda b, h: (0, 0))
    return pl.pallas_call(
        _attn_kernel,
        grid=grid,
        in_specs=[
            pl.BlockSpec((1, 1, s, LANES), lambda b, h: (b, h, 0, 0)),
            pl.BlockSpec((1, 1, LANES, LANES), lambda b, h: (0, h, 0, 0)),
            pl.BlockSpec((1, 1, s, LANES), lambda b, h: (b, h, 0, 0)),
            pl.BlockSpec((1, 1, LANES, LANES), lambda b, h: (0, h, 0, 0)),
            pl.BlockSpec((1, 1, VALUE_DIM, s), lambda b, h: (b, h, 0, 0)),
            pl.BlockSpec((1, 1, VALUE_DIM, LANES), lambda b, h: (0, h, 0, 0)),
            vec(HEAD_DIM), vec(HEAD_DIM), vec(HEAD_DIM), vec(HEAD_DIM),
            pl.BlockSpec((VALUE_DIM, 1), lambda b, h: (0, 0)),
        ],
        out_specs=[pl.BlockSpec((1, s, VALUE_DIM), lambda b, h: (b, 0, h)),
                   pl.BlockSpec((1, LANES, VALUE_DIM), lambda b, h: (b, 0, h))],
        out_shape=[jax.ShapeDtypeStruct((nb, s, ATTN_WIDTH), BF16),
                   jax.ShapeDtypeStruct((nb, LANES, ATTN_WIDTH), BF16)],
        compiler_params=_cparams(2),
        name="diff_attn",
    )(q, qm, kx, km, vxt, vmt, *lams, sg)


def _pool_kernel(pm_ref, px_ref, pw_ref, ps_ref, ox_ref, om_ref, seq_ref, *, seq_len):
    lp = seq_ref.shape[0]
    s = px_ref.shape[1]
    seq_ref[0:POOL_PAD, :] = jnp.zeros((POOL_PAD, POOL_WIDTH), F32)
    seq_ref[POOL_PAD + seq_len:lp, :] = jnp.zeros((lp - POOL_PAD - seq_len, POOL_WIDTH), F32)
    seq_ref[POOL_PAD:POOL_PAD + N_META, :] = pm_ref[...]
    seq_ref[POOL_PAD + N_META:POOL_PAD + seq_len, :] = px_ref[0]
    t = lax.broadcasted_iota(jnp.int32, (lp, 1), 0) - POOL_PAD
    for g, w in enumerate(POOL_WINDOWS):
        c0, c1 = g * POOL_GROUP_DIM, (g + 1) * POOL_GROUP_DIM
        sg = seq_ref[:, c0:c1]
        win = sg + pltpu.roll(sg, 1, 0)
        span = 1
        while 2 * span < w:
            win = pltpu.roll(win, span, 0) + pltpu.roll(win, lp - span, 0)
            span *= 2
        cnt = jnp.minimum(t + w // 2, seq_len) - jnp.maximum(t - w // 2, 0)
        cnt = jnp.maximum(cnt, 1).astype(F32)
        d = (win / cnt - sg).astype(BF16)
        y = _dot(d, pw_ref[g]) * ps_ref[:, c0:c1]
        om_ref[0, :, c0:c1] = y[POOL_PAD:POOL_PAD + N_META].astype(BF16)
        ox_ref[0, :, c0:c1] = y[POOL_PAD + N_META:POOL_PAD + N_META + s].astype(BF16)


def _pool(p_m, p_x, pool_w, pool_scale):
    nb, s, _ = p_x.shape
    seq_len = s + N_META
    lp = seq_len + 2 * POOL_PAD
    return pl.pallas_call(
        functools.partial(_pool_kernel, seq_len=seq_len),
        grid=(nb,),
        in_specs=[
            pl.BlockSpec((N_META, POOL_WIDTH), lambda b: (0, 0)),
            pl.BlockSpec((1, s, POOL_WIDTH), lambda b: (b, 0, 0)),
            pl.BlockSpec((len(POOL_WINDOWS), POOL_GROUP_DIM, POOL_GROUP_DIM), lambda b: (0, 0, 0)),
            pl.BlockSpec((1, POOL_WIDTH), lambda b: (0, 0)),
        ],
        out_specs=[pl.BlockSpec((1, s, POOL_WIDTH), lambda b: (b, 0, 0)),
                   pl.BlockSpec((1, N_META, POOL_WIDTH), lambda b: (b, 0, 0))],
        out_shape=[jax.ShapeDtypeStruct((nb, s, POOL_WIDTH), BF16),
                   jax.ShapeDtypeStruct((nb, N_META, POOL_WIDTH), BF16)],
        scratch_shapes=[pltpu.VMEM((lp, POOL_WIDTH), F32)],
        compiler_params=_cparams(1),
        name="pool_mixer",
    )(p_m, p_x, pool_w, pool_scale)


def _outproj_kernel(a_ref, p_ref, x_ref, wo_ref, g2_ref, wrh_ref, wrl_ref,
                    h_ref, hn_ref, aff_ref):
    mix = _dot(a_ref[0], wo_ref[0:ATTN_WIDTH, :]) + _dot(p_ref[0], wo_ref[ATTN_WIDTH:, :])
    h = x_ref[0] + mix
    h_ref[0] = h
    ms = jnp.mean(h * h, axis=-1, keepdims=True)
    hn = h * lax.rsqrt(ms + EPS) * g2_ref[...]
    for g in range(hn.shape[0] // 8):
        for j in range(LANE_TILES):
            r0 = (g * LANE_TILES + j) * 8
            hn_ref[0, r0:r0 + 8, :] = hn[g * 8:(g + 1) * 8, j * LANES:(j + 1) * LANES]
    hi = hn.astype(BF16)
    lo = (hn - hi.astype(F32)).astype(BF16)
    wrh = wrh_ref[...]
    logits = _dot_nt(wrh, hi) + _dot_nt(wrl_ref[...], hi) + _dot_nt(wrh, lo)
    m = jnp.max(logits, axis=0, keepdims=True)
    e = jnp.exp(logits - m)
    aff_ref[0] = e / jnp.sum(e, axis=0, keepdims=True)


def _out_proj(attn, pool, x, w_out, g2, wr_hi, wr_lo, tm):
    nb, n, _ = x.shape
    grid = (nb, n // tm)
    const = lambda shape: pl.BlockSpec(shape, lambda b, i: (0,) * len(shape))
    return pl.pallas_call(
        _outproj_kernel,
        grid=grid,
        in_specs=[
            pl.BlockSpec((1, tm, ATTN_WIDTH), lambda b, i: (b, i, 0)),
            pl.BlockSpec((1, tm, POOL_WIDTH), lambda b, i: (b, i, 0)),
            pl.BlockSpec((1, tm, D_MODEL), lambda b, i: (b, i, 0)),
            const((D_MODEL, D_MODEL)),
            const((1, D_MODEL)),
            const((N_EXPERTS, D_MODEL)),
            const((N_EXPERTS, D_MODEL)),
        ],
        out_specs=[pl.BlockSpec((1, tm, D_MODEL), lambda b, i: (b, i, 0)),
                   pl.BlockSpec((1, tm * LANE_TILES, LANES), lambda b, i: (b, i, 0)),
                   pl.BlockSpec((1, N_EXPERTS, tm), lambda b, i: (b, 0, i))],
        out_shape=[jax.ShapeDtypeStruct((nb, n, D_MODEL), F32),
                   jax.ShapeDtypeStruct((nb, n * LANE_TILES, LANES), F32),
                   jax.ShapeDtypeStruct((nb, N_EXPERTS, n), F32)],
        compiler_params=_cparams(2),
        name="out_proj_router",
    )(attn, pool, x, w_out, g2, wr_hi, wr_lo)


def _select_kernel(aff_ref, u_ref, slot_ref, win_ref, *, cap, n_tok):
    aff = aff_ref[...]
    rows, lpad = aff.shape

    def body(i, thr):
        cand = thr | jnp.left_shift(jnp.int32(1), 30 - i)
        cand_f = lax.bitcast_convert_type(cand, F32)
        cnt = jnp.sum((aff >= cand_f).astype(jnp.int32), axis=1, keepdims=True)
        return jnp.where(cnt >= cap, cand, thr)

    thr = lax.fori_loop(0, 31, body, jnp.zeros((rows, 1), jnp.int32))
    thr_f = lax.bitcast_convert_type(thr, F32)
    gt = aff > thr_f
    eq = aff == thr_f
    n_gt = jnp.sum(gt.astype(jnp.int32), axis=1, keepdims=True)
    u = u_ref[...]
    rank_eq = _dot(eq.astype(BF16), u)
    sel = gt | (eq & (rank_eq < (cap - n_gt).astype(F32)))
    pos = _dot(sel.astype(BF16), u)
    slot = jnp.where(sel, pos.astype(jnp.int32), -1)
    slot_ref[...] = slot

    lane = lax.broadcasted_iota(jnp.int32, (rows, lpad), 1)
    out_lane = lax.broadcasted_iota(jnp.int32, (rows, LANES), 1)
    win = jnp.zeros((rows, LANES), jnp.int32)
    n_tiles = n_tok // COMBINE_TILE
    big = jnp.int32(1 << 20)
    for j in range(n_tiles):
        t0 = N_META + j * COMBINE_TILE
        inside = (lane >= t0) & (lane < t0 + COMBINE_TILE) & sel
        lo = jnp.min(jnp.where(inside, slot, big), axis=1, keepdims=True)
        hi = jnp.max(jnp.where(inside, slot, -1), axis=1, keepdims=True)
        first = jnp.where(lo == big, 0, (lo // SLOT_ALIGN) * SLOT_ALIGN)
        win = jnp.where(out_lane == j, first, win)
        win = jnp.where(out_lane == WIN_HI_LANE + j, hi, win)
    win_ref[...] = win


def _select(aff_rows, u, cap, rb, n_tok):
    r, lpad = aff_rows.shape
    assert n_tok // COMBINE_TILE <= WIN_HI_LANE
    return pl.pallas_call(
        functools.partial(_select_kernel, cap=cap, n_tok=n_tok),
        grid=(r // rb,),
        in_specs=[pl.BlockSpec((rb, lpad), lambda i: (i, 0)),
                  pl.BlockSpec((lpad, lpad), lambda i: (0, 0))],
        out_specs=[pl.BlockSpec((rb, lpad), lambda i: (i, 0)),
                   pl.BlockSpec((rb, LANES), lambda i: (i, 0))],
        out_shape=[jax.ShapeDtypeStruct((r, lpad), jnp.int32),
                   jax.ShapeDtypeStruct((r, LANES), jnp.int32)],
        compiler_params=_cparams(1),
        name="expert_choice_select",
    )(aff_rows, u)


def _slot_values(slot_row, value_row):
    s = slot_row.shape[1]
    c = lax.broadcasted_iota(jnp.int32, (SLOT_ROWS, LANES), 0)
    acc = jnp.zeros((SLOT_ROWS, LANES), F32)
    for t0 in range(0, s, LANES):
        hit = slot_row[:, t0:t0 + LANES] == c
        acc = jnp.where(hit, value_row[:, t0:t0 + LANES], acc)
    return jnp.sum(acc, axis=1, keepdims=True)


def _slot_tokens_kernel(slot_ref, idx_ref):
    s = slot_ref.shape[2]
    tok = lax.broadcasted_iota(jnp.int32, (1, s), 1).astype(F32)
    for e in range(N_EXPERTS):
        idx_col = _slot_values(slot_ref[e], tok)
        head = jnp.broadcast_to(idx_col[0:MXU_DIM], (MXU_DIM, LANES)).T[0:1]
        tail = jnp.broadcast_to(idx_col[SLOT_ROWS - LANES:SLOT_ROWS], (LANES, LANES)).T[0:1]
        idx_ref[e, :, 0:MXU_DIM] = head.astype(jnp.int32)
        idx_ref[e, :, MXU_DIM:SLOT_LANES] = pltpu.roll(tail, SLOT_ROWS - MXU_DIM, 1).astype(jnp.int32)


def _slot_tokens(slot3):
    r, _, s = slot3.shape
    nb = r // N_EXPERTS
    rows = pl.BlockSpec((N_EXPERTS, 1, s), lambda b: (b, 0, 0))
    return pl.pallas_call(
        _slot_tokens_kernel,
        grid=(nb,),
        in_specs=[rows],
        out_specs=pl.BlockSpec((N_EXPERTS, 1, SLOT_LANES), lambda b: (b, 0, 0)),
        out_shape=jax.ShapeDtypeStruct((r, 1, SLOT_LANES), jnp.int32),
        compiler_params=_cparams(1),
        name="slot_tokens",
    )(slot3)


def _expert_kernel(idx_ref, slot_ref, aff_ref, hn_ref, wg32_ref, wu32_ref, wd32_ref, y_ref,
                   wg_s, wu_s, wd_s, rows_s):
    p = pl.program_id(0)
    b = pl.program_id(1)
    nb = pl.num_programs(1)
    fill = p % 2
    cur = (p * nb + b) % 2

    def gather_row(c):
        t = idx_ref[0, 0, c]
        first = lax.shift_right_logical(t, 3) * (8 * LANE_TILES) + (t & 7)
        rows_s[1 - cur, pl.ds(c, LANE_TILES, stride=GATHER_STRIDE), :] = hn_ref[0, pl.ds(first, LANE_TILES, stride=8), :]

    def gather_next():
        for c in range(SLOT_ROWS):
            gather_row(c)

    @pl.when(p < N_EXPERTS)
    def _():
        rg = wg32_ref.shape[1]
        rd = wd32_ref.shape[1]
        wg_s[fill, pl.ds(pl.multiple_of(b * rg, rg), rg), :] = wg32_ref[0].astype(BF16)
        wu_s[fill, pl.ds(pl.multiple_of(b * rg, rg), rg), :] = wu32_ref[0].astype(BF16)
        wd_s[fill, pl.ds(pl.multiple_of(b * rd, rd), rd), :] = wd32_ref[0].astype(BF16)

    @pl.when(p == 0)
    def _():
        y_ref[0] = jnp.zeros(y_ref.shape[1:], BF16)

    @pl.when((p == 0) & (b == nb - 1))
    def _():
        def body(c, carry):
            gather_row(c)
            return carry
        lax.fori_loop(0, SLOT_ROWS, body, 0)

    @pl.when(p > 0)
    def _():
        use = 1 - fill
        xg = jnp.concatenate([rows_s[cur, pl.ds(j * GATHER_STRIDE, SLOT_ROWS), :] for j in range(LANE_TILES)],
                             axis=1).astype(BF16)
        gather_next()
        g = _dot(xg, wg_s[use])
        u = _dot(xg, wu_s[use])
        a = (g * jax.nn.sigmoid(g) * u).astype(BF16)
        gate = _slot_values(slot_ref[0], aff_ref[0])
        y_ref[0] = (_dot(a, wd_s[use]) * gate).astype(BF16)


def _experts(idx3, slot3, aff3, hn_words, wg, wu, wd):
    nb = hn_words.shape[0]
    s = slot3.shape[2]
    assert D_MODEL % nb == 0 and (D_MODEL // nb) % 16 == 0
    rg = D_MODEL // nb
    rd = D_FF // nb
    grid = (N_EXPERTS + 1, nb)
    last = N_EXPERTS - 1

    def next_step(p, b):
        t = p * nb + b + 1
        return t % nb, jnp.clip(t // nb - 1, 0, last)

    def next_rows(p, b):
        b1, e1 = next_step(p, b)
        return b1 * N_EXPERTS + e1, 0, 0

    wmap = lambda p, b: (jnp.minimum(p, last), jnp.where(p <= last, b, nb - 1), 0)
    row = pl.BlockSpec((1, 1, s), lambda p, b: (b * N_EXPERTS + jnp.maximum(p - 1, 0), 0, 0))
    return pl.pallas_call(
        _expert_kernel,
        grid=grid,
        in_specs=[
            pl.BlockSpec((1, 1, SLOT_LANES), next_rows, memory_space=pltpu.SMEM),
            row, row,
            pl.BlockSpec((1,) + hn_words.shape[1:], lambda p, b: (next_step(p, b)[0], 0, 0)),
            pl.BlockSpec((1, rg, D_FF), wmap),
            pl.BlockSpec((1, rg, D_FF), wmap),
            pl.BlockSpec((1, rd, D_MODEL), wmap),
        ],
        out_specs=pl.BlockSpec((1, SLOT_ROWS, D_MODEL), lambda p, b: (b, p, 0)),
        out_shape=jax.ShapeDtypeStruct((nb, (N_EXPERTS + 1) * SLOT_ROWS, D_MODEL), BF16),
        scratch_shapes=[pltpu.VMEM((2, D_MODEL, D_FF), BF16),
                        pltpu.VMEM((2, D_MODEL, D_FF), BF16),
                        pltpu.VMEM((2, D_FF, D_MODEL), BF16),
                        pltpu.VMEM((2, GATHER_ROWS, LANES), F32)],
        compiler_params=_cparams(2),
        name="routed_experts",
    )(idx3, slot3, aff3, hn_words, wg, wu, wd)


def _combine_kernel(win_ref, h_ref, st_ref, y_ref, o_ref):
    j = pl.program_id(1)
    st = st_ref[0]
    tm = st.shape[0]
    lane = lax.broadcasted_iota(jnp.int32, (tm, COMBINE_WINDOW), 1)
    firsts = [win_ref[0, e, j] for e in range(N_EXPERTS)]
    n_pass = jnp.int32(1)
    for e in range(N_EXPERTS):
        span = win_ref[0, e, WIN_HI_LANE + j] - firsts[e]
        n_pass = jnp.maximum(n_pass, span // COMBINE_WINDOW + 1)

    per_dot = MXU_DIM // COMBINE_WINDOW

    def one_pass(k, acc):
        for e0 in range(0, N_EXPERTS, per_dot):
            hits, rows = [], []
            for e in range(e0, e0 + per_dot):
                fresh = firsts[e] + k * COMBINE_WINDOW
                start = pl.multiple_of(jnp.minimum(fresh, SLOT_ROWS - COMBINE_WINDOW), SLOT_ALIGN)
                rows.append(y_ref[0, pl.ds(pl.multiple_of((e + 1) * SLOT_ROWS + start, SLOT_ALIGN),
                                           COMBINE_WINDOW), :])
                col = st[:, e:e + 1]
                hits.append(((col == start + lane) & (col >= fresh)).astype(BF16))
            acc = acc + _dot(jnp.concatenate(hits, axis=1), jnp.concatenate(rows, axis=0))
        return acc

    o_ref[0] = lax.fori_loop(0, n_pass, one_pass, h_ref[0])


def _combine(h, slot_t, win, y):
    nb, s, _ = h.shape
    tm = COMBINE_TILE
    return pl.pallas_call(
        _combine_kernel,
        grid=(nb, s // tm),
        in_specs=[
            pl.BlockSpec((1, N_EXPERTS, LANES), lambda b, i: (b, 0, 0), memory_space=pltpu.SMEM),
            pl.BlockSpec((1, tm, D_MODEL), lambda b, i: (b, i, 0)),
            pl.BlockSpec((1, tm, N_EXPERTS), lambda b, i: (b, i, 0)),
            pl.BlockSpec((1, (N_EXPERTS + 1) * SLOT_ROWS, D_MODEL), lambda b, i: (b, 0, 0)),
        ],
        out_specs=pl.BlockSpec((1, tm, D_MODEL), lambda b, i: (b, i, 0)),
        out_shape=jax.ShapeDtypeStruct((nb, s, D_MODEL), F32),
        compiler_params=_cparams(2),
        name="moe_combine",
    )(win, h, slot_t, y)


def _rope_tables(pos):
    half = HEAD_DIM // 2
    freqs = ROPE_THETA ** (-jnp.arange(half, dtype=F32) / half)
    ang = pos[:, None] * freqs[None, :]
    cos = jnp.cos(ang)
    sin = jnp.sin(ang)
    cos_t = jnp.concatenate([cos, cos, cos, cos], axis=1)
    sin_t = jnp.concatenate([-sin, sin, -sin, sin], axis=1)
    return cos_t, sin_t


def kernel(x, meta_tokens, ln1_g, w_in, q_norm_g, k_norm_g, lambda_q1, lambda_k1, lambda_q2, lambda_k2,
           subln_g, pool_w, pool_scale, w_out, ln2_g, w_router, w_gate, w_up, w_down):
    nb, s, _ = x.shape
    seq_len = s + N_META
    cap = CAPACITY_FACTOR * seq_len // N_EXPERTS
    assert cap <= SLOT_ROWS and s % COMBINE_TILE == 0
    l = 0

    w_in_b = w_in[l].astype(BF16)
    w_out_b = w_out[l].astype(BF16)
    pw_b = pool_w[l].astype(BF16)
    wr_t = w_router[l].T
    wr_hi = wr_t.astype(BF16)
    wr_lo = (wr_t - wr_hi.astype(F32)).astype(BF16)
    g1 = ln1_g[l][None]
    g2 = ln2_g[l][None]
    qg = jnp.tile(q_norm_g[l], 2)[None]
    kg = jnp.tile(k_norm_g[l], 2)[None]
    sg = subln_g[l][:, None]
    ps = pool_scale[l][None]
    lams = (lambda_q1[l][None], lambda_k1[l][None], lambda_q2[l][None], lambda_k2[l][None])
    lane = jnp.arange(MXU_DIM)
    gm = (lane[:, None] // HEAD_DIM == lane[None, :] // HEAD_DIM).astype(BF16)
    cos_x, sin_x = _rope_tables(jnp.arange(N_META, seq_len, dtype=F32))
    cos_m, sin_m = _rope_tables(jnp.arange(N_META, dtype=F32))

    tm = _row_tile(s, 1024)
    wvt_b = w_in_b[:, 2 * ATTN_WIDTH:3 * ATTN_WIDTH].T
    q_x, k_x, vt_x, p_x = _in_proj(x, g1, w_in_b, wvt_b, cos_x, sin_x, qg, kg, gm, tm)
    q_m, k_m, vt_m, p_m = _in_proj(meta_tokens[None].astype(x.dtype), g1, w_in_b, wvt_b, cos_m, sin_m,
                                   qg, kg, gm, N_META)
    pad_rows = ((0, 0), (0, 0), (0, LANES - N_META), (0, 0))
    q_mp = jnp.pad(q_m, pad_rows)
    k_mp = jnp.pad(k_m, pad_rows)
    vt_mp = jnp.pad(vt_m, ((0, 0), (0, 0), (0, 0), (0, LANES - N_META)))

    attn_x, attn_m = _attention(q_x, q_mp, k_x, k_mp, vt_x, vt_mp, lams, sg)
    attn_m = attn_m[:, :N_META]

    pool_x, pool_m = _pool(p_m[0], p_x, pw_b, ps)

    h_x, hn_x, aff_x = _out_proj(attn_x, pool_x, x, w_out_b, g2, wr_hi, wr_lo, tm)
    n_mrows = nb * N_META
    meta_rows = jnp.broadcast_to(meta_tokens[None].astype(x.dtype), (nb, N_META, D_MODEL)).reshape(1, n_mrows, D_MODEL)
    _, _, aff_m = _out_proj(attn_m.reshape(1, n_mrows, ATTN_WIDTH), pool_m.reshape(1, n_mrows, POOL_WIDTH),
                            meta_rows, w_out_b, g2, wr_hi, wr_lo, n_mrows)
    aff_m = aff_m.reshape(N_EXPERTS, nb, N_META).transpose(1, 0, 2)

    lpad = -(-seq_len // LANES) * LANES
    aff_seq = jnp.concatenate([aff_m, aff_x], axis=2)
    aff_rows = jnp.pad(aff_seq, ((0, 0), (0, 0), (0, lpad - seq_len)), constant_values=-1.0)
    aff_rows = aff_rows.reshape(nb * N_EXPERTS, lpad)
    idx = jnp.arange(lpad)
    upper = (idx[:, None] < idx[None, :]).astype(BF16)
    slot, win = _select(aff_rows, upper, cap, min(nb * N_EXPERTS, 128), s)
    slot_x = slot[:, N_META:seq_len]

    slot3 = slot_x.reshape(nb * N_EXPERTS, 1, s)
    y = _experts(_slot_tokens(slot3), slot3, aff_x.reshape(nb * N_EXPERTS, 1, s),
                 hn_x, w_gate[l], w_up[l], w_down[l])

    slot_t = slot_x.reshape(nb, N_EXPERTS, s).transpose(0, 2, 1)
    return _combine(h_x, slot_t, win.reshape(nb, N_EXPERTS, LANES), y)
```
